```python
import jax
import jax.numpy as jnp
from jax import lax

D_MODEL = 1024
BATCH = 32
SEQ = 2048
DEPTH = 4

CTX_LEN = 256
GRID_W = 64
N_MIXERS = 3
N_CONV_LAYERS = (DEPTH + 2) // N_MIXERS
N_RWKV_LAYERS = (DEPTH + 1) // N_MIXERS
N_ATTN_LAYERS = DEPTH // N_MIXERS
CONV_WIDTH = 31
RWKV_HEAD = 64
RWKV_HEADS = D_MODEL // RWKV_HEAD
DECAY_LORA = 64
ICLR_LORA = 64
ATTN_HEAD = 128
ATTN_Q_HEADS = D_MODEL // 64
ATTN_KV_HEADS = ATTN_Q_HEADS // 2
ATTN_GROUP = ATTN_Q_HEADS // ATTN_KV_HEADS
ATTN_QW = ATTN_Q_HEADS * ATTN_HEAD
ATTN_KVW = ATTN_KV_HEADS * ATTN_HEAD
Q_BLOCK = 128
ROPE_THETA = 10000.0
AXIS_DIM = ATTN_HEAD // 2
NORM_EPS = 1e-6
LN_EPS = 1e-5
GN_EPS = 64e-5

kernel_name = "hybrid_conv_rwkv7_gqa_prefix_dit"


def rmsnorm(x, g):
    x32 = x.astype(jnp.float32)
    y = x32 * lax.rsqrt(jnp.mean(x32 * x32, axis=-1, keepdims=True) + NORM_EPS)
    return (y * g.astype(jnp.float32)).astype(x.dtype)


def layernorm(x, g, b):
    x32 = x.astype(jnp.float32)
    mean = jnp.mean(x32, axis=-1, keepdims=True)
    var = jnp.mean(jnp.square(x32 - mean), axis=-1, keepdims=True)
    y = (x32 - mean) * lax.rsqrt(var + LN_EPS)
    return (y * g.astype(jnp.float32) + b.astype(jnp.float32)).astype(x.dtype)


def modulation(cvec, w, b):
    m = jax.nn.silu(cvec) @ w + b
    return jnp.split(m, 3, axis=-1)


def modulate(n, shift, scale):
    return n * (1.0 + scale) + shift


def rope_tables(T):
    rows = T // GRID_W
    row = jnp.repeat(jnp.arange(rows), GRID_W).astype(jnp.float32)
    col = (jnp.arange(rows * GRID_W) % GRID_W).astype(jnp.float32)
    inv = 1.0 / (ROPE_THETA ** (jnp.arange(0, AXIS_DIM, 2, dtype=jnp.float32) / AXIS_DIM))
    ang = jnp.stack([row[:, None] * inv, col[:, None] * inv])
    return jnp.cos(ang), jnp.sin(ang)


def rope_axis(x, cos, sin):
    x1, x2 = jnp.split(x, 2, axis=-1)
    return jnp.concatenate([x1 * cos - x2 * sin, x2 * cos + x1 * sin], axis=-1)


def rope2d(x, cos, sin):
    y = jnp.concatenate([rope_axis(x[..., :AXIS_DIM], cos[0], sin[0]),
                         rope_axis(x[..., AXIS_DIM:], cos[1], sin[1])], axis=-1)
    return y.astype(x.dtype)


def depthwise_conv(y, w, b):
    C = y.shape[-1]
    out = lax.conv_general_dilated(y, w[:, None, :].astype(y.dtype), window_strides=(1,),
                                   padding=((CONV_WIDTH // 2, CONV_WIDTH // 2),),
                                   dimension_numbers=("NWC", "WIO", "NWC"),
                                   feature_group_count=C)
    return out + b


def conv_branch(h, w_in, dw, db, ln_g, ln_b, w_out):
    a, b, gate = jnp.split(h @ w_in, 3, axis=-1)
    y = a * jax.nn.sigmoid(b)
    y = depthwise_conv(y, dw, db)
    y = jax.nn.silu(layernorm(y, ln_g, ln_b))
    return (y * jax.nn.silu(gate)) @ w_out


def centred_shift(h):
    hp = jnp.pad(h, ((0, 0), (1, 1), (0, 0)))
    return 0.5 * (hp[:, :-2] + hp[:, 2:])


def rwkv_features(h, proj, with_out):
    mu, w_r, w_k, w_v, w_g, w0, w1, w2, a0, a1, a2, k_k, k_a = proj
    B, T, _ = h.shape
    heads = lambda t: t.reshape(B, T, RWKV_HEADS, RWKV_HEAD)
    xx = centred_shift(h) - h
    lerp = lambda n: h + xx * mu[n]
    k = lerp(2) @ w_k
    v = heads(lerp(3) @ w_v)
    kk = heads(k * k_k).astype(jnp.float32)
    kk = kk * lax.rsqrt(jnp.sum(kk * kk, axis=-1, keepdims=True) + 1e-12)
    xw, xa = lerp(1), lerp(4)
    dirs = []
    for d in range(2):
        w_log = -jax.nn.softplus(-(w0[d] + jnp.tanh(xw @ w1[d]) @ w2[d])) - 0.5
        dec = jnp.exp(-jnp.exp(w_log.astype(jnp.float32)))
        a = jax.nn.sigmoid(a0[d] + (xa @ a1[d]) @ a2[d])
        dirs.append((heads(dec), heads(k * (1.0 + (a - 1.0) * k_a)), heads(a)))
    r = heads(lerp(0) @ w_r) if with_out else None
    g = (lerp(5) @ w_g) if with_out else None
    return r, g, v, kk, dirs


def wkv_scan(state0, r, dec, k, v, kk, a, reverse):
    emit = r is not None

    def step(S, inp):
        d_t, k_t, v_t, kk_t, a_t = inp[:5]
        sa = jnp.einsum("bhvk,bhk->bhv", S, -kk_t)
        S = (S * d_t[:, :, None, :] + sa[..., None] * (kk_t * a_t)[:, :, None, :]
             + v_t[..., None] * k_t[:, :, None, :])
        out = jnp.einsum("bhvk,bhk->bhv", S, inp[5]) if emit else None
        return S, out

    xs = (dec, k, v, kk, a, r) if emit else (dec, k, v, kk, a)
    return lax.scan(step, state0, xs, reverse=reverse)


def rwkv_readout(o, r, v, k_sum, g, r_k, ln_g, ln_b, w_o):
    o = jnp.moveaxis(o, 0, 1)
    B, T = o.shape[:2]
    mean = jnp.mean(o, axis=-1, keepdims=True)
    var = jnp.mean(jnp.square(o - mean), axis=-1, keepdims=True)
    o = (o - mean) * lax.rsqrt(var + GN_EPS)
    o = o * ln_g.reshape(RWKV_HEADS, RWKV_HEAD) + ln_b.reshape(RWKV_HEADS, RWKV_HEAD)
    bonus = jnp.sum(r * k_sum * r_k, axis=-1, keepdims=True) * v
    y = (o + bonus).astype(g.dtype).reshape(B, T, D_MODEL)
    return (y * jax.nn.silu(g)) @ w_o


def rwkv_mixer(h, hc, mu, w_r, w_k, w_v, w_g, w0, w1, w2, a0, a1, a2, k_k, k_a,
               r_k, ln_g, ln_b, w_o, ctx_out):
    proj = (mu, w_r, w_k, w_v, w_g, w0, w1, w2, a0, a1, a2, k_k, k_a)
    r, g, v, kk, dirs = rwkv_features(h, proj, True)
    rc, gc, vc, kkc, dirs_c = rwkv_features(hc, proj, ctx_out)
    tm = lambda t: None if t is None else jnp.moveaxis(t, 1, 0)
    state0 = jnp.zeros((h.shape[0], RWKV_HEADS, RWKV_HEAD, RWKV_HEAD), jnp.float32)
    o, oc = 0.0, 0.0
    for d, rev in ((0, False), (1, True)):
        dec_c, k_c, a_c = dirs_c[d]
        state_c, o_c_d = wkv_scan(state0, tm(rc), tm(dec_c), tm(k_c), tm(vc), tm(kkc), tm(a_c), rev)
        dec_l, k_l, a_l = dirs[d]
        _, o_d = wkv_scan(state_c, tm(r), tm(dec_l), tm(k_l), tm(v), tm(kk), tm(a_l), rev)
        o = o + o_d
        if ctx_out:
            oc = oc + o_c_d
    y = rwkv_readout(o, r, v, dirs[0][1] + dirs[1][1], g, r_k, ln_g, ln_b, w_o)
    yc = None
    if ctx_out:
        yc = rwkv_readout(oc, rc, vc, dirs_c[0][1] + dirs_c[1][1], gc, r_k, ln_g, ln_b, w_o)
    return y, yc


def split_heads(t, n):
    B, T, _ = t.shape
    return t.reshape(B, T, n, ATTN_HEAD).transpose(0, 2, 1, 3)


def merge_heads(o):
    B, _, _, Q, _ = o.shape
    return o.transpose(0, 3, 1, 2, 4).reshape(B, Q, ATTN_QW)


def gqa_attend(q, k, v):
    s = jnp.einsum("bhgqd,bhkd->bhgqk", q, k) * (ATTN_HEAD ** -0.5)
    return jnp.einsum("bhgqk,bhkd->bhgqd", jax.nn.softmax(s, axis=-1), v)


def blocked_attention(q, k, v):
    B, _, T, _ = q.shape
    nb = T // Q_BLOCK
    qb = q.astype(jnp.float32).reshape(B, ATTN_KV_HEADS, ATTN_GROUP, nb, Q_BLOCK, ATTN_HEAD)
    qb = qb.transpose(3, 0, 1, 2, 4, 5)
    ob = lax.map(lambda qq: gqa_attend(qq, k, v), qb)
    return ob.transpose(1, 0, 4, 2, 3, 5).reshape(B, T, ATTN_QW)


def attn_mixer(h, hc, w_in, q_g, k_g, w_out, cos, sin, ctx_out):
    cuts = [ATTN_QW, ATTN_QW + ATTN_KVW, ATTN_QW + 2 * ATTN_KVW]
    q, k, v, g = jnp.split(h @ w_in, cuts, axis=-1)
    q = rope2d(rmsnorm(split_heads(q, ATTN_Q_HEADS), q_g), cos, sin)
    k = rope2d(rmsnorm(split_heads(k, ATTN_KV_HEADS), k_g), cos, sin)
    v = split_heads(v, ATTN_KV_HEADS)
    if ctx_out:
        qc, kc, vc, gc = jnp.split(hc @ w_in, cuts, axis=-1)
    else:
        kc, vc = jnp.split(hc @ w_in[:, ATTN_QW:ATTN_QW + 2 * ATTN_KVW], 2, axis=-1)
    kc = rmsnorm(split_heads(kc, ATTN_KV_HEADS), k_g).astype(jnp.float32)
    vc = split_heads(vc, ATTN_KV_HEADS).astype(jnp.float32)
    k_all = jnp.concatenate([kc, k.astype(jnp.float32)], axis=2)
    v_all = jnp.concatenate([vc, v.astype(jnp.float32)], axis=2)
    o = blocked_attention(q, k_all, v_all).astype(g.dtype)
    y = (o * jax.nn.silu(g)) @ w_out
    yc = None
    if ctx_out:
        qc = rmsnorm(split_heads(qc, ATTN_Q_HEADS), q_g).astype(jnp.float32)
        B, _, L, _ = qc.shape
        oc = merge_heads(gqa_attend(qc.reshape(B, ATTN_KV_HEADS, ATTN_GROUP, L, ATTN_HEAD), kc, vc))
        yc = (oc.astype(gc.dtype) * jax.nn.silu(gc)) @ w_out
    return y, yc


def setup_inputs(seed: int = 0) -> dict:
    key = jax.random.key(seed)
    keys = iter(jax.random.split(key, 48))
    D, NC, NR, NA = D_MODEL, N_CONV_LAYERS, N_RWKV_LAYERS, N_ATTN_LAYERS

    def nrm(shape, s):
        return jax.random.normal(next(keys), shape, jnp.float32) * s

    def unif(shape, lo, hi):
        return jax.random.uniform(next(keys), shape, jnp.float32, lo, hi)

    return {
        "x": nrm((BATCH, SEQ, D), 1.0),
        "c": nrm((BATCH, D), 1.0),
        "ctx": nrm((BATCH, CTX_LEN, D), 1.0),
        "c_ctx": nrm((D,), 1.0),
        "norm_g": 1.0 + nrm((DEPTH, D), 0.02),
        "mod_w": nrm((DEPTH, D, 3 * D), D ** -0.5),
        "mod_b": nrm((DEPTH, 3 * D), 0.02),
        "conv_w_in": nrm((NC, D, 3 * D), D ** -0.5),
        "conv_dw": nrm((NC, CONV_WIDTH, D), CONV_WIDTH ** -0.5),
        "conv_db": nrm((NC, D), 0.02),
        "conv_ln_g": 1.0 + nrm((NC, D), 0.02),
        "conv_ln_b": nrm((NC, D), 0.02),
        "conv_w_out": nrm((NC, D, D), D ** -0.5),
        "rwkv_mu": unif((NR, 6, D), 0.0, 1.0),
        "rwkv_w_r": nrm((NR, D, D), D ** -0.5),
        "rwkv_w_k": nrm((NR, D, D), D ** -0.5),
        "rwkv_w_v": nrm((NR, D, D), D ** -0.5),
        "rwkv_w_g": nrm((NR, D, D), D ** -0.5),
        "rwkv_w0": unif((NR, 2, D), -5.0, 0.5),
        "rwkv_w1": nrm((NR, 2, D, DECAY_LORA), D ** -0.5),
        "rwkv_w2": nrm((NR, 2, DECAY_LORA, D), 0.1 * DECAY_LORA ** -0.5),
        "rwkv_a0": nrm((NR, 2, D), 0.1),
        "rwkv_a1": nrm((NR, 2, D, ICLR_LORA), D ** -0.5),
        "rwkv_a2": nrm((NR, 2, ICLR_LORA, D), 0.1 * ICLR_LORA ** -0.5),
        "rwkv_k_k": 0.85 + nrm((NR, D), 0.02),
        "rwkv_k_a": 1.0 + nrm((NR, D), 0.02),
        "rwkv_r_k": nrm((NR, RWKV_HEADS, RWKV_HEAD), 0.1),
        "rwkv_ln_g": 1.0 + nrm((NR, D), 0.02),
        "rwkv_ln_b": nrm((NR, D), 0.02),
        "rwkv_w_o": nrm((NR, D, D), D ** -0.5),
        "attn_w_in": nrm((NA, D, 2 * ATTN_QW + 2 * ATTN_KVW), D ** -0.5),
        "attn_q_g": 1.0 + nrm((NA, ATTN_HEAD), 0.02),
        "attn_k_g": 1.0 + nrm((NA, ATTN_HEAD), 0.02),
        "attn_w_out": nrm((NA, ATTN_QW, D), ATTN_QW ** -0.5),
        "final_g": 1.0 + nrm((D,), 0.02),
    }


def reference(x, c, ctx, c_ctx, norm_g, mod_w, mod_b,
              conv_w_in, conv_dw, conv_db, conv_ln_g, conv_ln_b, conv_w_out,
              rwkv_mu, rwkv_w_r, rwkv_w_k, rwkv_w_v, rwkv_w_g,
              rwkv_w0, rwkv_w1, rwkv_w2, rwkv_a0, rwkv_a1, rwkv_a2,
              rwkv_k_k, rwkv_k_a, rwkv_r_k, rwkv_ln_g, rwkv_ln_b, rwkv_w_o,
              attn_w_in, attn_q_g, attn_k_g, attn_w_out, final_g):
    cos, sin = rope_tables(x.shape[1])
    xc = ctx
    for i in range(DEPTH):
        kind, j = i % N_MIXERS, i // N_MIXERS
        ctx_out = any(l % N_MIXERS != 0 for l in range(i + 1, DEPTH))
        ctx_in = ctx_out or kind != 0
        shift, scale, gate = modulation(c, mod_w[i], mod_b[i])
        h = modulate(rmsnorm(x, norm_g[i]), shift[:, None], scale[:, None])
        hc = None
        if ctx_in:
            shift_c, scale_c, gate_c = modulation(c_ctx, mod_w[i], mod_b[i])
            hc = modulate(rmsnorm(xc, norm_g[i]), shift_c, scale_c)
        if kind == 0:
            conv_p = (conv_w_in[j], conv_dw[j], conv_db[j], conv_ln_g[j], conv_ln_b[j], conv_w_out[j])
            y = conv_branch(h, *conv_p)
            yc = conv_branch(hc, *conv_p) if ctx_out else None
        elif kind == 1:
            y, yc = rwkv_mixer(h, hc, rwkv_mu[j], rwkv_w_r[j], rwkv_w_k[j], rwkv_w_v[j], rwkv_w_g[j],
                               rwkv_w0[j], rwkv_w1[j], rwkv_w2[j], rwkv_a0[j], rwkv_a1[j], rwkv_a2[j],
                               rwkv_k_k[j], rwkv_k_a[j], rwkv_r_k[j], rwkv_ln_g[j], rwkv_ln_b[j],
                               rwkv_w_o[j], ctx_out)
        else:
            y, yc = attn_mixer(h, hc, attn_w_in[j], attn_q_g[j], attn_k_g[j], attn_w_out[j],
                               cos, sin, ctx_out)
        x = x + gate[:, None] * y
        if ctx_out:
            xc = xc + gate_c * yc
    return rmsnorm(x, final_g)
```

```python
import functools
import math

import jax
import jax.numpy as jnp
from jax import lax
from jax.experimental import pallas as pl
from jax.experimental.pallas import tpu as pltpu

N_MIXERS = 3
RWKV_HEAD = 64
ATTN_HEAD = 128
ATTN_GROUP = 2
GRID_W = 64
ROPE_THETA = 10000.0
NORM_EPS = 1e-6
LN_EPS = 1e-5
GN_EPS = 64e-5

V7X_VMEM_BYTES = 64 * 1024 * 1024
VMEM_LIMIT_BYTES = V7X_VMEM_BYTES - 8 * 1024 * 1024
LANES = 128
SCAN_CHUNK = 64
HALO = 16

F32 = jnp.float32
BF16 = jnp.bfloat16
HI = lax.Precision.HIGHEST


def _cparams(*sem):
    return pltpu.CompilerParams(dimension_semantics=sem, vmem_limit_bytes=VMEM_LIMIT_BYTES)


def _const_spec(shape):
    nd = len(shape)
    return pl.BlockSpec(shape, lambda *_: (0,) * nd, pipeline_mode=pl.Buffered(1))


def _mod_spec(arr):
    nb, _, d = arr.shape
    if nb == 1:
        return pl.BlockSpec((1, 1, d), lambda b, i: (0, 0, 0))
    return pl.BlockSpec((1, 1, d), lambda b, i: (b, 0, 0))


def _sigmoid(x):
    return 1.0 / (1.0 + jnp.exp(-x))


def _silu(x):
    return x * _sigmoid(x)


def _norm_mod(x, g, shift, scale):
    ms = jnp.mean(x * x, axis=-1, keepdims=True)
    n = x * lax.rsqrt(ms + NORM_EPS) * g
    return n * (1.0 + scale) + shift


def _dot(a, b):
    return jnp.dot(a, b, preferred_element_type=F32)


def _dot_hi(a, b):
    return jnp.dot(a, b, preferred_element_type=F32, precision=HI)


def _iota_div(shape, axis, n):
    assert n & (n - 1) == 0
    return lax.shift_right_logical(lax.broadcasted_iota(jnp.int32, shape, axis), n.bit_length() - 1)


def _iota_mod(shape, axis, n):
    assert n & (n - 1) == 0
    return lax.broadcasted_iota(jnp.int32, shape, axis) & (n - 1)


def _head_ones(width, head):
    r = _iota_div((width, width), 0, head)
    c = _iota_div((width, width), 1, head)
    return jnp.where(r == c, 1.0, 0.0).astype(F32)


def _modulation_body(c_ref, w_ref, b_ref, o_ref):
    s = _silu(c_ref[...])
    o_ref[0] = _dot_hi(s, w_ref[0]) + b_ref[0]


def _modulation(cond, mod_w, mod_b):
    depth, d, d3 = mod_w.shape
    rows = cond.shape[0]
    return pl.pallas_call(
        _modulation_body,
        grid=(depth, d3 // d),
        in_specs=[
            pl.BlockSpec((rows, d), lambda i, j: (0, 0)),
            pl.BlockSpec((1, d, d), lambda i, j: (i, 0, j)),
            pl.BlockSpec((1, 1, d), lambda i, j: (i, 0, j)),
        ],
        out_specs=pl.BlockSpec((1, rows, d), lambda i, j: (i, 0, j)),
        out_shape=jax.ShapeDtypeStruct((depth, rows, d3), F32),
        compiler_params=_cparams("parallel", "parallel"),
        name="modulation",
    )(cond, mod_w, mod_b.reshape(depth, 1, d3))


def _conv_in_body(x_ref, g_ref, sh_ref, sc_ref, w_ref, y_ref, sg_ref, *, tn):
    d = x_ref.shape[-1]
    h = _norm_mod(x_ref[0], g_ref[...], sh_ref[0], sc_ref[0]).astype(BF16)
    for n in range(d // tn):
        cols = slice(n * tn, (n + 1) * tn)
        a = _dot(h, w_ref[0, :, cols])
        b = _dot(h, w_ref[1, :, cols])
        gt = _dot(h, w_ref[2, :, cols])
        y_ref[0, :, cols] = a * _sigmoid(b)
        sg_ref[0, :, cols] = _silu(gt)


def _conv_in(x, g, shift, scale, w3, *, tm):
    bsz, t, d = x.shape
    tn = min(d, 256)
    row = pl.BlockSpec((1, tm, d), lambda b, i: (b, i, 0))
    return pl.pallas_call(
        functools.partial(_conv_in_body, tn=tn),
        grid=(bsz, t // tm),
        in_specs=[row, _const_spec((1, d)), _mod_spec(shift), _mod_spec(scale), _const_spec((3, d, d))],
        out_specs=[row, row],
        out_shape=[jax.ShapeDtypeStruct((bsz, t, d), F32)] * 2,
        compiler_params=_cparams("parallel", "parallel"),
        name="conv_in",
    )(x, g, shift, scale, w3)


def _conv_mid_body(yp_ref, yc_ref, yn_ref, sg_ref, x_ref, dw_ref, db_ref, lg_ref, lb_ref, w_ref, gate_ref,
                   *rest, width, final):
    if final:
        fg_ref, o_ref = rest
    else:
        (o_ref,) = rest
    i = pl.program_id(1)
    nt = pl.num_programs(1)
    tq = yc_ref.shape[1]
    prev = jnp.where(i > 0, yp_ref[0], 0.0)
    nxt = jnp.where(i < nt - 1, yn_ref[0], 0.0)
    win = jnp.concatenate([prev, yc_ref[0], nxt], axis=0)
    off = HALO - width // 2
    acc = win[off:off + tq] * dw_ref[0:1, :]
    for k in range(1, width):
        acc = acc + win[off + k:off + k + tq] * dw_ref[k:k + 1, :]
    acc = acc + db_ref[...]
    mean = jnp.mean(acc, axis=-1, keepdims=True)
    cen = acc - mean
    var = jnp.mean(cen * cen, axis=-1, keepdims=True)
    yn = cen * lax.rsqrt(var + LN_EPS) * lg_ref[...] + lb_ref[...]
    u = (_silu(yn) * sg_ref[0]).astype(BF16)
    out = x_ref[0] + gate_ref[0] * _dot(u, w_ref[...])
    if final:
        ms = jnp.mean(out * out, axis=-1, keepdims=True)
        out = out * lax.rsqrt(ms + NORM_EPS) * fg_ref[...]
    o_ref[0] = out


def _conv_mid(y, sg, x, dw, db, ln_g, ln_b, w_out, gate, final_g, *, tq):
    bsz, t, d = x.shape
    width = dw.shape[0]
    assert width // 2 <= HALO and t % HALO == 0 and tq % HALO == 0
    r = tq // HALO
    nh = t // HALO
    row = pl.BlockSpec((1, tq, d), lambda b, i: (b, i, 0))
    prev = pl.BlockSpec((1, HALO, d), lambda b, i: (b, jnp.maximum(i * r - 1, 0), 0))
    nxt = pl.BlockSpec((1, HALO, d), lambda b, i: (b, jnp.minimum((i + 1) * r, nh - 1), 0))
    final = final_g is not None
    in_specs = [prev, row, nxt, row, row, _const_spec((width, d)), _const_spec((1, d)), _const_spec((1, d)),
                _const_spec((1, d)), _const_spec((d, d)), _mod_spec(gate)]
    args = [y, y, y, sg, x, dw, db, ln_g, ln_b, w_out, gate]
    if final:
        in_specs.append(_const_spec((1, d)))
        args.append(final_g)
    return pl.pallas_call(
        functools.partial(_conv_mid_body, width=width, final=final),
        grid=(bsz, t // tq),
        in_specs=in_specs,
        out_specs=row,
        out_shape=jax.ShapeDtypeStruct((bsz, t, d), F32),
        compiler_params=_cparams("parallel", "parallel"),
        name="conv_mid",
    )(*args)


def _rwkv_feat_body(xp_ref, xc_ref, xn_ref, g_ref, sh_ref, sc_ref, mu_ref, wr_ref, wk_ref, wv_ref, wg_ref,
                    w1_ref, w2_ref, a1_ref, a2_ref, w0_ref, a0_ref, kk_ref_in, ka_ref,
                    r_ref, v_ref, g_out_ref, kk_ref, lw0_ref, lw1_ref, b0_ref, b1_ref, kd0_ref, kd1_ref):
    i = pl.program_id(1)
    nt = pl.num_programs(1)
    tm, d = xc_ref.shape[1], xc_ref.shape[2]
    g, sh, sc = g_ref[...], sh_ref[0], sc_ref[0]
    h = _norm_mod(xc_ref[0], g, sh, sc)
    hp = _norm_mod(xp_ref[0], g, sh, sc)[7:8]
    hn = _norm_mod(xn_ref[0], g, sh, sc)[0:1]
    hp = jnp.where(i > 0, hp, 0.0)
    hn = jnp.where(i < nt - 1, hn, 0.0)
    rows = lax.broadcasted_iota(jnp.int32, (tm, d), 0)
    hm1 = jnp.where(rows == 0, hp, pltpu.roll(h, 1, 0))
    hp1 = jnp.where(rows == tm - 1, hn, pltpu.roll(h, tm - 1, 0))
    xx = 0.5 * (hm1 + hp1) - h

    def lerp(n):
        return (h + xx * mu_ref[n:n + 1, :]).astype(BF16)

    r = _dot(lerp(0), wr_ref[...])
    k = _dot(lerp(2), wk_ref[...])
    v = _dot(lerp(3), wv_ref[...])
    r_ref[0] = r
    v_ref[0] = v
    g_out_ref[0] = _dot(lerp(5), wg_ref[...])
    kx = k * kk_ref_in[...]
    ones = _head_ones(LANES, RWKV_HEAD)
    parts = []
    for c in range(d // LANES):
        sl = slice(c * LANES, (c + 1) * LANES)
        q = kx[:, sl]
        parts.append(q * lax.rsqrt(_dot_hi(q * q, ones) + 1e-12))
    kk = parts[0] if len(parts) == 1 else jnp.concatenate(parts, axis=1)
    kk_ref[0] = kk
    tw = jnp.tanh(_dot(lerp(1), w1_ref[...])).astype(BF16)
    ta = _dot(lerp(4), a1_ref[...]).astype(BF16)
    for dr, (lw_ref, b_ref, kd_ref) in enumerate(((lw0_ref, b0_ref, kd0_ref), (lw1_ref, b1_ref, kd1_ref))):
        z = w0_ref[dr:dr + 1, :] + _dot(tw, w2_ref[dr])
        sp = jnp.maximum(-z, 0.0) + jnp.log(1.0 + jnp.exp(-jnp.abs(z)))
        lw_ref[0] = -jnp.exp(-sp - 0.5)
        a = _sigmoid(a0_ref[dr:dr + 1, :] + _dot(ta, a2_ref[dr]))
        b_ref[0] = kk * a
        kd_ref[0] = k * (1.0 + (a - 1.0) * ka_ref[...])


def _rwkv_feat(x, g, shift, scale, p, *, tm):
    bsz, t, d = x.shape
    assert t % 8 == 0 and tm % 8 == 0
    r8 = tm // 8
    n8 = t // 8
    row = pl.BlockSpec((1, tm, d), lambda b, i: (b, i, 0))
    prev = pl.BlockSpec((1, 8, d), lambda b, i: (b, jnp.maximum(i * r8 - 1, 0), 0))
    nxt = pl.BlockSpec((1, 8, d), lambda b, i: (b, jnp.minimum((i + 1) * r8, n8 - 1), 0))
    lora = p["w1"].shape[-1]
    in_specs = [prev, row, nxt, _const_spec((1, d)), _mod_spec(shift), _mod_spec(scale), _const_spec((6, d)),
                _const_spec((d, d)), _const_spec((d, d)), _const_spec((d, d)), _const_spec((d, d)),
                _const_spec((d, lora)), _const_spec((2, lora, d)), _const_spec((d, lora)), _const_spec((2, lora, d)),
                _const_spec((2, d)), _const_spec((2, d)), _const_spec((1, d)), _const_spec((1, d))]
    return pl.pallas_call(
        _rwkv_feat_body,
        grid=(bsz, t // tm),
        in_specs=in_specs,
        out_specs=[row] * 10,
        out_shape=[jax.ShapeDtypeStruct((bsz, t, d), F32)] * 10,
        compiler_params=_cparams("parallel", "parallel"),
        name="rwkv_feat",
    )(x, x, x, g, shift, scale, p["mu"], p["w_r"], p["w_k"], p["w_v"], p["w_g"],
      p["w1"], p["w2"], p["a1"], p["a2"], p["w0"], p["a0"], p["k_k"], p["k_a"])


def _scan_body(lw_ref, kk_ref, b_ref, kd_ref, v_ref, r_ref, s0_ref, o_ref, s1_ref, h_ref, *, reverse, gl):
    c = pl.program_id(1)
    nc = pl.num_programs(1)
    cs_len, d = lw_ref.shape[1], lw_ref.shape[2]
    n = RWKV_HEAD
    assert cs_len == n, "block-diagonal packing assumes chunk == head size"
    hp = gl // n

    @pl.when(c == 0)
    def _():
        h_ref[...] = s0_ref[0]

    ri = lax.broadcasted_iota(jnp.int32, (cs_len, cs_len), 0)
    ci = lax.broadcasted_iota(jnp.int32, (cs_len, cs_len), 1)
    tri = jnp.where((ci >= ri) if reverse else (ci <= ri), 1.0, 0.0).astype(F32)
    lw = lw_ref[0]
    cum = _dot_hi(tri, lw)
    tot = cum[0:1] if reverse else cum[cs_len - 1:cs_len]
    g_incl = jnp.exp(cum)
    g_excl = jnp.exp(cum - lw)
    g_inv = jnp.exp(-cum)
    g_tot = jnp.exp(tot)
    kkt = kk_ref[0] * g_excl
    kh = kd_ref[0] * g_inv
    bh = b_ref[0] * g_inv
    rt = r_ref[0] * g_incl
    kb = kh * g_tot
    bb = bh * g_tot
    vv = v_ref[0]

    bdmask = _iota_div((gl, gl), 0, n) == _iota_div((gl, gl), 1, n)
    i_s = lax.broadcasted_iota(jnp.int32, (cs_len, gl), 0)
    j_s = _iota_mod((cs_len, gl), 1, n)
    strict = (j_s > i_s) if reverse else (j_s < i_s)
    incl = (j_s >= i_s) if reverse else (j_s <= i_s)
    eye = j_s == i_s
    lane_head = _iota_div((n, gl), 1, n)

    def bd(y):
        return jnp.where(bdmask, jnp.concatenate([y] * hp, axis=0), 0.0)

    def mm_t(a, bm):
        return lax.dot_general(a, bm, (((1,), (1,)), ((), ())), precision=HI, preferred_element_type=F32)

    def mm_c0(a, bm):
        return lax.dot_general(a, bm, (((0,), (0,)), ((), ())), precision=HI, preferred_element_type=F32)

    def collapse(m):
        out = jnp.where(lane_head == 0, m[0:n], 0.0)
        for hh in range(1, hp):
            out = out + jnp.where(lane_head == hh, m[hh * n:(hh + 1) * n], 0.0)
        return out

    for g in range(d // gl):
        sl = slice(g * gl, (g + 1) * gl)
        lhs = jnp.concatenate([kkt[:, sl], rt[:, sl]], axis=0)
        ab = mm_t(lhs, bd(bh[:, sl]))
        ak = mm_t(lhs, bd(kh[:, sl]))
        a_m = jnp.where(strict, ab[:cs_len], 0.0)
        m2 = jnp.where(incl, ab[cs_len:], 0.0)
        b_m = jnp.where(strict, ak[:cs_len], 0.0)
        m1 = jnp.where(incl, ak[cs_len:], 0.0)
        xp = a_m
        tm_ = jnp.where(eye, 1.0, 0.0) - a_m
        for _ in range(int(math.log2(cs_len)) - 1):
            xp = _dot_hi(xp, bd(xp))
            tm_ = tm_ + _dot_hi(tm_, bd(xp))
        v_g = vv[:, sl]
        bdv = bd(v_g)
        wt = _dot_hi(tm_, bd(kkt[:, sl]))
        u0 = _dot_hi(tm_, bd(_dot_hi(b_m, bdv)))
        rbar = rt[:, sl] - _dot_hi(m2, bd(wt))
        o0 = _dot_hi(m1, bdv) - _dot_hi(m2, bd(u0))
        pc = mm_c0(bb[:, sl], wt)
        gc = mm_c0(kb[:, sl], v_g) - mm_c0(bb[:, sl], u0)
        p_s = jnp.where(eye, g_tot[:, sl], 0.0) - collapse(pc)
        g_s = collapse(gc)
        bdh = bd(h_ref[:, sl])
        o_ref[0, :, sl] = _dot_hi(rbar, bdh) + o0
        h_ref[:, sl] = _dot_hi(p_s, bdh) + g_s

    @pl.when(c == nc - 1)
    def _():
        s1_ref[0] = h_ref[...]


def _rwkv_scan(lw, kk, b, kd, v, r, state, *, reverse):
    bsz, t, d = lw.shape
    cs = SCAN_CHUNK
    nc = t // cs
    gl = min(d, 256)
    if reverse:
        row = pl.BlockSpec((1, cs, d), lambda bi, c: (bi, nc - 1 - c, 0))
    else:
        row = pl.BlockSpec((1, cs, d), lambda bi, c: (bi, c, 0))
    st = pl.BlockSpec((1, RWKV_HEAD, d), lambda bi, c: (bi, 0, 0))
    return pl.pallas_call(
        functools.partial(_scan_body, reverse=reverse, gl=gl),
        grid=(bsz, nc),
        in_specs=[row] * 6 + [st],
        out_specs=[row, st],
        out_shape=[jax.ShapeDtypeStruct((bsz, t, d), F32), jax.ShapeDtypeStruct((bsz, RWKV_HEAD, d), F32)],
        scratch_shapes=[pltpu.VMEM((RWKV_HEAD, d), F32)],
        compiler_params=_cparams("parallel", "arbitrary"),
        name="rwkv_scan_rev" if reverse else "rwkv_scan_fwd",
    )(lw, kk, b, kd, v, r, state)


def _rwkv_out_body(o0_ref, o1_ref, r_ref, kd0_ref, kd1_ref, v_ref, g_ref, x_ref, rk_ref, lg_ref, lb_ref, w_ref,
                   gate_ref, out_ref, y_ref):
    d = x_ref.shape[-1]
    ones = _head_ones(LANES, RWKV_HEAD)
    inv_n = 1.0 / RWKV_HEAD
    for c in range(d // LANES):
        sl = slice(c * LANES, (c + 1) * LANES)
        o = o0_ref[0, :, sl] + o1_ref[0, :, sl]
        mean = _dot_hi(o, ones) * inv_n
        cen = o - mean
        var = _dot_hi(cen * cen, ones) * inv_n
        on = cen * lax.rsqrt(var + GN_EPS) * lg_ref[:, sl] + lb_ref[:, sl]
        ksum = kd0_ref[0, :, sl] + kd1_ref[0, :, sl]
        bonus = _dot_hi(r_ref[0, :, sl] * ksum * rk_ref[:, sl], ones) * v_ref[0, :, sl]
        y_ref[:, sl] = ((on + bonus) * _silu(g_ref[0, :, sl])).astype(BF16)
    out_ref[0] = x_ref[0] + gate_ref[0] * _dot(y_ref[...], w_ref[...])


def _rwkv_out(o0, o1, r, kd0, kd1, v, g, x, r_k, ln_g, ln_b, w_o, gate, *, tm):
    bsz, t, d = x.shape
    row = pl.BlockSpec((1, tm, d), lambda b, i: (b, i, 0))
    return pl.pallas_call(
        _rwkv_out_body,
        grid=(bsz, t // tm),
        in_specs=[row] * 8 + [_const_spec((1, d))] * 3 + [_const_spec((d, d)), _mod_spec(gate)],
        out_specs=row,
        out_shape=jax.ShapeDtypeStruct((bsz, t, d), F32),
        scratch_shapes=[pltpu.VMEM((tm, d), BF16)],
        compiler_params=_cparams("parallel", "parallel"),
        name="rwkv_out",
    )(o0, o1, r, kd0, kd1, v, g, x, r_k, ln_g, ln_b, w_o, gate)


def _head_norm(x, g):
    ms = jnp.mean(x * x, axis=-1, keepdims=True)
    return x * lax.rsqrt(ms + NORM_EPS) * g


def _rope(x, cos, sin_signed):
    lane = _iota_mod(x.shape, 1, ATTN_HEAD // 2)
    swapped = jnp.where(lane < ATTN_HEAD // 4, pltpu.roll(x, ATTN_HEAD - ATTN_HEAD // 4, 1),
                        pltpu.roll(x, ATTN_HEAD // 4, 1))
    return x * cos + swapped * sin_signed


def _attn_in_body(x_ref, g_ref, sh_ref, sc_ref, w_ref, qg_ref, kg_ref, cos_ref, sin_ref, *outs,
                  qw, kvw, with_q, rope):
    h = _norm_mod(x_ref[0], g_ref[...], sh_ref[0], sc_ref[0]).astype(BF16)
    hd = ATTN_HEAD
    if with_q:
        q_ref, k_ref, v_ref, gt_ref = outs
        base_k = qw
    else:
        k_ref, v_ref = outs
        base_k = 0
    cos, sin = cos_ref[...], sin_ref[...]
    if with_q:
        for n in range(qw // hd):
            q = _head_norm(_dot(h, w_ref[:, n * hd:(n + 1) * hd]), qg_ref[...])
            q_ref[0, :, n * hd:(n + 1) * hd] = _rope(q, cos, sin).astype(BF16)
    for n in range(kvw // hd):
        k = _head_norm(_dot(h, w_ref[:, base_k + n * hd:base_k + (n + 1) * hd]), kg_ref[...])
        if rope:
            k = _rope(k, cos, sin)
        k_ref[0, :, n * hd:(n + 1) * hd] = k.astype(BF16)
    tn = 2 * hd
    for n in range(kvw // tn):
        v_ref[0, :, n * tn:(n + 1) * tn] = _dot(
            h, w_ref[:, base_k + kvw + n * tn:base_k + kvw + (n + 1) * tn]).astype(BF16)
    if with_q:
        for n in range(qw // tn):
            gt = _dot(h, w_ref[:, qw + 2 * kvw + n * tn:qw + 2 * kvw + (n + 1) * tn])
            gt_ref[0, :, n * tn:(n + 1) * tn] = _silu(gt)


def _attn_in(x, g, shift, scale, w, q_g, k_g, cos, sin, *, qw, kvw, with_q, rope, tm):
    bsz, t, d = x.shape
    row = pl.BlockSpec((1, tm, d), lambda b, i: (b, i, 0))
    tab = pl.BlockSpec((tm, ATTN_HEAD), lambda b, i: (i, 0))
    nw = w.shape[1]
    outs = []
    shapes = []
    if with_q:
        outs.append(pl.BlockSpec((1, tm, qw), lambda b, i: (b, i, 0)))
        shapes.append(jax.ShapeDtypeStruct((bsz, t, qw), BF16))
    for _ in range(2):
        outs.append(pl.BlockSpec((1, tm, kvw), lambda b, i: (b, i, 0)))
        shapes.append(jax.ShapeDtypeStruct((bsz, t, kvw), BF16))
    if with_q:
        outs.append(pl.BlockSpec((1, tm, qw), lambda b, i: (b, i, 0)))
        shapes.append(jax.ShapeDtypeStruct((bsz, t, qw), F32))
    return pl.pallas_call(
        functools.partial(_attn_in_body, qw=qw, kvw=kvw, with_q=with_q, rope=rope),
        grid=(bsz, t // tm),
        in_specs=[row, _const_spec((1, d)), _mod_spec(shift), _mod_spec(scale), _const_spec((d, nw)),
                  _const_spec((1, ATTN_HEAD)), _const_spec((1, ATTN_HEAD)), tab, tab],
        out_specs=outs,
        out_shape=shapes,
        compiler_params=_cparams("parallel", "parallel"),
        name="attn_in" if with_q else "attn_in_ctx",
    )(x, g, shift, scale, w, q_g, k_g, cos, sin)


def _flash_body(q_ref, kc_ref, vc_ref, k_ref, v_ref, o_ref):
    scale = ATTN_HEAD ** -0.5
    kc, vc, k, v = kc_ref[0], vc_ref[0], k_ref[0], v_ref[0]
    nt = (((1,), (1,)), ((), ()))
    for gq in range(ATTN_GROUP):
        sl = slice(gq * ATTN_HEAD, (gq + 1) * ATTN_HEAD)
        q = q_ref[0, :, sl]
        s_c = lax.dot_general(q, kc, nt, preferred_element_type=F32) * scale
        s_l = lax.dot_general(q, k, nt, preferred_element_type=F32) * scale
        m = jnp.maximum(jnp.max(s_c, axis=-1, keepdims=True), jnp.max(s_l, axis=-1, keepdims=True))
        p_c = jnp.exp(s_c - m)
        p_l = jnp.exp(s_l - m)
        den = jnp.sum(p_c, axis=-1, keepdims=True) + jnp.sum(p_l, axis=-1, keepdims=True)
        num = _dot(p_c.astype(BF16), vc) + _dot(p_l.astype(BF16), v)
        o_ref[0, :, sl] = num / den


def _flash(q, kc, vc, k, v, *, tq):
    bsz, t, qw = q.shape
    lc = kc.shape[1]
    hkv = k.shape[2] // ATTN_HEAD
    gw = ATTN_GROUP * ATTN_HEAD
    qspec = pl.BlockSpec((1, tq, gw), lambda b, hh, i: (b, i, hh))
    cspec = pl.BlockSpec((1, lc, ATTN_HEAD), lambda b, hh, i: (b, 0, hh))
    kspec = pl.BlockSpec((1, t, ATTN_HEAD), lambda b, hh, i: (b, 0, hh))
    return pl.pallas_call(
        _flash_body,
        grid=(bsz, hkv, t // tq),
        in_specs=[qspec, cspec, cspec, kspec, kspec],
        out_specs=qspec,
        out_shape=jax.ShapeDtypeStruct((bsz, t, qw), F32),
        compiler_params=_cparams("parallel", "parallel", "arbitrary"),
        name="flash_gqa",
    )(q, kc, vc, k, v)


def _gated_out_body(o_ref, sg_ref, x_ref, w_ref, gate_ref, out_ref):
    u = (o_ref[0] * sg_ref[0]).astype(BF16)
    out_ref[0] = x_ref[0] + gate_ref[0] * _dot(u, w_ref[...])


def _gated_out(o, sg, x, w, gate, *, tm):
    bsz, t, d = x.shape
    kdim = o.shape[-1]
    row = pl.BlockSpec((1, tm, d), lambda b, i: (b, i, 0))
    wide = pl.BlockSpec((1, tm, kdim), lambda b, i: (b, i, 0))
    return pl.pallas_call(
        _gated_out_body,
        grid=(bsz, t // tm),
        in_specs=[wide, wide, row, _const_spec((kdim, d)), _mod_spec(gate)],
        out_specs=row,
        out_shape=jax.ShapeDtypeStruct((bsz, t, d), F32),
        compiler_params=_cparams("parallel", "parallel"),
        name="attn_out",
    )(o, sg, x, w, gate)


def _final_norm_body(x_ref, g_ref, o_ref):
    x = x_ref[0]
    ms = jnp.mean(x * x, axis=-1, keepdims=True)
    o_ref[0] = x * lax.rsqrt(ms + NORM_EPS) * g_ref[...]


def _final_norm(x, g, *, tm):
    bsz, t, d = x.shape
    row = pl.BlockSpec((1, tm, d), lambda b, i: (b, i, 0))
    return pl.pallas_call(
        _final_norm_body,
        grid=(bsz, t // tm),
        in_specs=[row, _const_spec((1, d))],
        out_specs=row,
        out_shape=jax.ShapeDtypeStruct((bsz, t, d), F32),
        compiler_params=_cparams("parallel", "parallel"),
        name="final_norm",
    )(x, g)


def _rope_tables(t):
    axis_dim = ATTN_HEAD // 2
    rows = t // GRID_W
    row = jnp.repeat(jnp.arange(rows), GRID_W).astype(F32)
    col = (jnp.arange(rows * GRID_W) % GRID_W).astype(F32)
    inv = 1.0 / (ROPE_THETA ** (jnp.arange(0, axis_dim, 2, dtype=F32) / axis_dim))
    ar, ac = row[:, None] * inv, col[:, None] * inv
    cos = jnp.concatenate([jnp.cos(ar), jnp.cos(ar), jnp.cos(ac), jnp.cos(ac)], axis=-1)
    sin = jnp.concatenate([-jnp.sin(ar), jnp.sin(ar), -jnp.sin(ac), jnp.sin(ac)], axis=-1)
    return cos, sin


def _tile(t, want):
    tm = min(t, want)
    assert t % tm == 0
    return tm


def kernel(x, c, ctx, c_ctx, norm_g, mod_w, mod_b, conv_w_in, conv_dw, conv_db, conv_ln_g, conv_ln_b, conv_w_out, rwkv_mu, rwkv_w_r, rwkv_w_k, rwkv_w_v, rwkv_w_g, rwkv_w0, rwkv_w1, rwkv_w2, rwkv_a0, rwkv_a1, rwkv_a2, rwkv_k_k, rwkv_k_a, rwkv_r_k, rwkv_ln_g, rwkv_ln_b, rwkv_w_o, attn_w_in, attn_q_g, attn_k_g, attn_w_out, final_g):
    bsz, t, d = x.shape
    lc = ctx.shape[1]
    depth = mod_w.shape[0]
    qw = (d // 64) * ATTN_HEAD
    kvw = qw // ATTN_GROUP

    pad = (-(bsz + 1)) % 8
    cond = jnp.concatenate([c, c_ctx[None, :], jnp.zeros((pad, d), F32)], axis=0)
    mods = _modulation(cond, mod_w, mod_b)
    cos, sin = _rope_tables(t)
    final_row = final_g.reshape(1, d)

    xc = ctx
    for i in range(depth):
        kind, j = i % N_MIXERS, i // N_MIXERS
        ctx_out = any(l % N_MIXERS != 0 for l in range(i + 1, depth))
        ctx_in = ctx_out or kind != 0
        last = i == depth - 1
        m = mods[i]
        shift, scale, gate = (m[:bsz, None, k * d:(k + 1) * d] for k in range(3))
        shift_c, scale_c, gate_c = (m[bsz:bsz + 1, None, k * d:(k + 1) * d] for k in range(3))
        g = norm_g[i].reshape(1, d)
        if kind == 0:
            w3 = conv_w_in[j].reshape(d, 3, d).transpose(1, 0, 2).astype(BF16)
            w_out = conv_w_out[j].astype(BF16)
            cp = (conv_dw[j], conv_db[j].reshape(1, d), conv_ln_g[j].reshape(1, d), conv_ln_b[j].reshape(1, d), w_out)
            y, sg = _conv_in(x, g, shift, scale, w3, tm=_tile(t, 256))
            x_new = _conv_mid(y, sg, x, *cp, gate, final_row if last else None, tq=_tile(t, 256))
            if ctx_out:
                yc, sgc = _conv_in(xc, g, shift_c, scale_c, w3, tm=_tile(lc, 256))
                xc = _conv_mid(yc, sgc, xc, *cp, gate_c, None, tq=_tile(lc, 256))
            x = x_new
        elif kind == 1:
            lora = rwkv_w1.shape[-1]
            zero = jnp.zeros((lora, d), F32)

            def padded(w2):
                return jnp.stack([jnp.concatenate([w2[0], zero], 0), jnp.concatenate([zero, w2[1]], 0)]).astype(BF16)

            p = dict(
                mu=rwkv_mu[j], w_r=rwkv_w_r[j].astype(BF16), w_k=rwkv_w_k[j].astype(BF16),
                w_v=rwkv_w_v[j].astype(BF16), w_g=rwkv_w_g[j].astype(BF16),
                w1=jnp.concatenate([rwkv_w1[j, 0], rwkv_w1[j, 1]], axis=-1).astype(BF16), w2=padded(rwkv_w2[j]),
                a1=jnp.concatenate([rwkv_a1[j, 0], rwkv_a1[j, 1]], axis=-1).astype(BF16), a2=padded(rwkv_a2[j]),
                w0=rwkv_w0[j], a0=rwkv_a0[j], k_k=rwkv_k_k[j].reshape(1, d), k_a=rwkv_k_a[j].reshape(1, d))
            p["w1"] = p["w1"].reshape(d, 2 * lora)
            fl = _rwkv_feat(x, g, shift, scale, p, tm=_tile(t, 256))
            fc = _rwkv_feat(xc, g, shift_c, scale_c, p, tm=_tile(lc, 256))
            state0 = jnp.zeros((bsz, RWKV_HEAD, d), F32)
            o_l, o_c = [], []
            for dr, rev in ((0, False), (1, True)):
                oc_d, st = _rwkv_scan(fc[4 + dr], fc[3], fc[6 + dr], fc[8 + dr], fc[1], fc[0], state0, reverse=rev)
                ol_d, _ = _rwkv_scan(fl[4 + dr], fl[3], fl[6 + dr], fl[8 + dr], fl[1], fl[0], st, reverse=rev)
                o_l.append(ol_d)
                o_c.append(oc_d)
            ro = (rwkv_r_k[j].reshape(1, d), rwkv_ln_g[j].reshape(1, d), rwkv_ln_b[j].reshape(1, d),
                  rwkv_w_o[j].astype(BF16))
            x_new = _rwkv_out(o_l[0], o_l[1], fl[0], fl[8], fl[9], fl[1], fl[2], x, *ro, gate, tm=_tile(t, 256))
            if ctx_out:
                xc = _rwkv_out(o_c[0], o_c[1], fc[0], fc[8], fc[9], fc[1], fc[2], xc, *ro, gate_c, tm=_tile(lc, 256))
            x = x_new
        else:
            w_in = attn_w_in[j].astype(BF16)
            qg, kg = attn_q_g[j].reshape(1, ATTN_HEAD), attn_k_g[j].reshape(1, ATTN_HEAD)
            q, k, v, sg = _attn_in(x, g, shift, scale, w_in, qg, kg, cos, sin, qw=qw, kvw=kvw, with_q=True,
                                   rope=True, tm=_tile(t, 256))
            if ctx_out:
                raise NotImplementedError("context-stream attention output is not needed at this depth")
            no_rope = jnp.zeros((lc, ATTN_HEAD), F32)
            kc, vc = _attn_in(xc, g, shift_c, scale_c, w_in[:, qw:qw + 2 * kvw], qg, kg, no_rope, no_rope,
                              qw=qw, kvw=kvw, with_q=False, rope=False, tm=_tile(lc, 256))
            o = _flash(q, kc, vc, k, v, tq=_tile(t, 256))
            x = _gated_out(o, sg, x, attn_w_out[j].astype(BF16), gate, tm=_tile(t, 256))
        if last and kind != 0:
            x = _final_norm(x, final_row, tm=_tile(t, 256))
    return x
```

```python
import functools
import math

import jax
import jax.numpy as jnp
from jax import lax
from jax.experimental import pallas as pl
from jax.experimental.pallas import tpu as pltpu

N_MIXERS = 3
RWKV_HEAD = 64
ATTN_HEAD = 128
ATTN_GROUP = 2
GRID_W = 64
ROPE_THETA = 10000.0
NORM_EPS = 1e-6
LN_EPS = 1e-5
GN_EPS = 64e-5

V7X_VMEM_BYTES = 64 * 1024 * 1024
VMEM_LIMIT_BYTES = V7X_VMEM_BYTES - 8 * 1024 * 1024
LANES = 128
SCAN_CHUNK = 64
HALO = 16

F32 = jnp.float32
BF16 = jnp.bfloat16
HI = lax.Precision.HIGHEST


def _cparams(*sem):
    return pltpu.CompilerParams(dimension_semantics=sem, vmem_limit_bytes=VMEM_LIMIT_BYTES)


def _const_spec(shape):
    nd = len(shape)
    return pl.BlockSpec(shape, lambda *_: (0,) * nd, pipeline_mode=pl.Buffered(1))


def _mod_spec(arr):
    nb, _, d = arr.shape
    if nb == 1:
        return pl.BlockSpec((1, 1, d), lambda b, i: (0, 0, 0))
    return pl.BlockSpec((1, 1, d), lambda b, i: (b, 0, 0))


def _sigmoid(x):
    return 1.0 / (1.0 + jnp.exp(-x))


def _silu(x):
    return x * _sigmoid(x)


def _norm_mod(x, g, shift, scale):
    ms = jnp.mean(x * x, axis=-1, keepdims=True)
    n = x * lax.rsqrt(ms + NORM_EPS) * g
    return n * (1.0 + scale) + shift


def _dot(a, b):
    return jnp.dot(a, b, preferred_element_type=F32)


def _dot_hi(a, b):
    return jnp.dot(a, b, preferred_element_type=F32, precision=HI)


def _iota_div(shape, axis, n):
    assert n & (n - 1) == 0
    return lax.shift_right_logical(lax.broadcasted_iota(jnp.int32, shape, axis), n.bit_length() - 1)


def _iota_mod(shape, axis, n):
    assert n & (n - 1) == 0
    return lax.broadcasted_iota(jnp.int32, shape, axis) & (n - 1)


def _head_ones(width, head):
    r = _iota_div((width, width), 0, head)
    c = _iota_div((width, width), 1, head)
    return jnp.where(r == c, 1.0, 0.0).astype(F32)


def _modulation_body(c_ref, w_ref, b_ref, o_ref):
    s = _silu(c_ref[...])
    o_ref[0] = _dot_hi(s, w_ref[0]) + b_ref[0]


def _modulation(cond, mod_w, mod_b):
    depth, d, d3 = mod_w.shape
    rows = cond.shape[0]
    return pl.pallas_call(
        _modulation_body,
        grid=(depth, d3 // d),
        in_specs=[
            pl.BlockSpec((rows, d), lambda i, j: (0, 0)),
            pl.BlockSpec((1, d, d), lambda i, j: (i, 0, j)),
            pl.BlockSpec((1, 1, d), lambda i, j: (i, 0, j)),
        ],
        out_specs=pl.BlockSpec((1, rows, d), lambda i, j: (i, 0, j)),
        out_shape=jax.ShapeDtypeStruct((depth, rows, d3), F32),
        compiler_params=_cparams("parallel", "parallel"),
        name="modulation",
    )(cond, mod_w, mod_b.reshape(depth, 1, d3))


def _conv_in_body(x_ref, g_ref, sh_ref, sc_ref, w_ref, y_ref, sg_ref, *, tn):
    d = x_ref.shape[-1]
    h = _norm_mod(x_ref[0], g_ref[...], sh_ref[0], sc_ref[0]).astype(BF16)
    for n in range(d // tn):
        cols = slice(n * tn, (n + 1) * tn)
        a = _dot(h, w_ref[0, :, cols])
        b = _dot(h, w_ref[1, :, cols])
        gt = _dot(h, w_ref[2, :, cols])
        y_ref[0, :, cols] = a * _sigmoid(b)
        sg_ref[0, :, cols] = _silu(gt)


def _conv_in(x, g, shift, scale, w3, *, tm):
    bsz, t, d = x.shape
    tn = min(d, 256)
    row = pl.BlockSpec((1, tm, d), lambda b, i: (b, i, 0))
    return pl.pallas_call(
        functools.partial(_conv_in_body, tn=tn),
        grid=(bsz, t // tm),
        in_specs=[row, _const_spec((1, d)), _mod_spec(shift), _mod_spec(scale), _const_spec((3, d, d))],
        out_specs=[row, row],
        out_shape=[jax.ShapeDtypeStruct((bsz, t, d), F32)] * 2,
        compiler_params=_cparams("parallel", "parallel"),
        name="conv_in",
    )(x, g, shift, scale, w3)


def _conv_mid_body(yp_ref, yc_ref, yn_ref, sg_ref, x_ref, dw_ref, db_ref, lg_ref, lb_ref, w_ref, gate_ref,
                   *rest, width, final):
    if final:
        fg_ref, o_ref = rest
    else:
        (o_ref,) = rest
    i = pl.program_id(1)
    nt = pl.num_programs(1)
    tq = yc_ref.shape[1]
    prev = jnp.where(i > 0, yp_ref[0], 0.0)
    nxt = jnp.where(i < nt - 1, yn_ref[0], 0.0)
    win = jnp.concatenate([prev, yc_ref[0], nxt], axis=0)
    off = HALO - width // 2
    acc = win[off:off + tq] * dw_ref[0:1, :]
    for k in range(1, width):
        acc = acc + win[off + k:off + k + tq] * dw_ref[k:k + 1, :]
    acc = acc + db_ref[...]
    mean = jnp.mean(acc, axis=-1, keepdims=True)
    cen = acc - mean
    var = jnp.mean(cen * cen, axis=-1, keepdims=True)
    yn = cen * lax.rsqrt(var + LN_EPS) * lg_ref[...] + lb_ref[...]
    u = (_silu(yn) * sg_ref[0]).astype(BF16)
    out = x_ref[0] + gate_ref[0] * _dot(u, w_ref[...])
    if final:
        ms = jnp.mean(out * out, axis=-1, keepdims=True)
        out = out * lax.rsqrt(ms + NORM_EPS) * fg_ref[...]
    o_ref[0] = out


def _conv_mid(y, sg, x, dw, db, ln_g, ln_b, w_out, gate, final_g, *, tq):
    bsz, t, d = x.shape
    width = dw.shape[0]
    assert width // 2 <= HALO and t % HALO == 0 and tq % HALO == 0
    r = tq // HALO
    nh = t // HALO
    row = pl.BlockSpec((1, tq, d), lambda b, i: (b, i, 0))
    prev = pl.BlockSpec((1, HALO, d), lambda b, i: (b, jnp.maximum(i * r - 1, 0), 0))
    nxt = pl.BlockSpec((1, HALO, d), lambda b, i: (b, jnp.minimum((i + 1) * r, nh - 1), 0))
    final = final_g is not None
    in_specs = [prev, row, nxt, row, row, _const_spec((width, d)), _const_spec((1, d)), _const_spec((1, d)),
                _const_spec((1, d)), _const_spec((d, d)), _mod_spec(gate)]
    args = [y, y, y, sg, x, dw, db, ln_g, ln_b, w_out, gate]
    if final:
        in_specs.append(_const_spec((1, d)))
        args.append(final_g)
    return pl.pallas_call(
        functools.partial(_conv_mid_body, width=width, final=final),
        grid=(bsz, t // tq),
        in_specs=in_specs,
        out_specs=row,
        out_shape=jax.ShapeDtypeStruct((bsz, t, d), F32),
        compiler_params=_cparams("parallel", "parallel"),
        name="conv_mid",
    )(*args)


def _rwkv_feat_body(xp_ref, xc_ref, xn_ref, g_ref, sh_ref, sc_ref, mu_ref, wr_ref, wk_ref, wv_ref, wg_ref,
                    w1_ref, w2_ref, a1_ref, a2_ref, w0_ref, a0_ref, kk_ref_in, ka_ref,
                    r_ref, v_ref, g_out_ref, kk_ref, lw0_ref, lw1_ref, b0_ref, b1_ref, kd0_ref, kd1_ref):
    i = pl.program_id(1)
    nt = pl.num_programs(1)
    tm, d = xc_ref.shape[1], xc_ref.shape[2]
    g, sh, sc = g_ref[...], sh_ref[0], sc_ref[0]
    h = _norm_mod(xc_ref[0], g, sh, sc)
    hp = _norm_mod(xp_ref[0], g, sh, sc)[7:8]
    hn = _norm_mod(xn_ref[0], g, sh, sc)[0:1]
    hp = jnp.where(i > 0, hp, 0.0)
    hn = jnp.where(i < nt - 1, hn, 0.0)
    rows = lax.broadcasted_iota(jnp.int32, (tm, d), 0)
    hm1 = jnp.where(rows == 0, hp, pltpu.roll(h, 1, 0))
    hp1 = jnp.where(rows == tm - 1, hn, pltpu.roll(h, tm - 1, 0))
    xx = 0.5 * (hm1 + hp1) - h

    def lerp(n):
        return (h + xx * mu_ref[n:n + 1, :]).astype(BF16)

    r = _dot(lerp(0), wr_ref[...])
    k = _dot(lerp(2), wk_ref[...])
    v = _dot(lerp(3), wv_ref[...])
    r_ref[0] = r
    v_ref[0] = v
    g_out_ref[0] = _dot(lerp(5), wg_ref[...])
    kx = k * kk_ref_in[...]
    ones = _head_ones(LANES, RWKV_HEAD)
    parts = []
    for c in range(d // LANES):
        sl = slice(c * LANES, (c + 1) * LANES)
        q = kx[:, sl]
        parts.append(q * lax.rsqrt(_dot_hi(q * q, ones) + 1e-12))
    kk = parts[0] if len(parts) == 1 else jnp.concatenate(parts, axis=1)
    kk_ref[0] = kk
    tw = jnp.tanh(_dot(lerp(1), w1_ref[...])).astype(BF16)
    ta = _dot(lerp(4), a1_ref[...]).astype(BF16)
    for dr, (lw_ref, b_ref, kd_ref) in enumerate(((lw0_ref, b0_ref, kd0_ref), (lw1_ref, b1_ref, kd1_ref))):
        z = w0_ref[dr:dr + 1, :] + _dot(tw, w2_ref[dr])
        sp = jnp.maximum(-z, 0.0) + jnp.log(1.0 + jnp.exp(-jnp.abs(z)))
        lw_ref[0] = -jnp.exp(-sp - 0.5)
        a = _sigmoid(a0_ref[dr:dr + 1, :] + _dot(ta, a2_ref[dr]))
        b_ref[0] = kk * a
        kd_ref[0] = k * (1.0 + (a - 1.0) * ka_ref[...])


def _rwkv_feat(x, g, shift, scale, p, *, tm):
    bsz, t, d = x.shape
    assert t % 8 == 0 and tm % 8 == 0
    r8 = tm // 8
    n8 = t // 8
    row = pl.BlockSpec((1, tm, d), lambda b, i: (b, i, 0))
    prev = pl.BlockSpec((1, 8, d), lambda b, i: (b, jnp.maximum(i * r8 - 1, 0), 0))
    nxt = pl.BlockSpec((1, 8, d), lambda b, i: (b, jnp.minimum((i + 1) * r8, n8 - 1), 0))
    lora = p["w1"].shape[-1]
    in_specs = [prev, row, nxt, _const_spec((1, d)), _mod_spec(shift), _mod_spec(scale), _const_spec((6, d)),
                _const_spec((d, d)), _const_spec((d, d)), _const_spec((d, d)), _const_spec((d, d)),
                _const_spec((d, lora)), _const_spec((2, lora, d)), _const_spec((d, lora)), _const_spec((2, lora, d)),
                _const_spec((2, d)), _const_spec((2, d)), _const_spec((1, d)), _const_spec((1, d))]
    return pl.pallas_call(
        _rwkv_feat_body,
        grid=(bsz, t // tm),
        in_specs=in_specs,
        out_specs=[row] * 10,
        out_shape=[jax.ShapeDtypeStruct((bsz, t, d), F32)] * 10,
        compiler_params=_cparams("parallel", "parallel"),
        name="rwkv_feat",
    )(x, x, x, g, shift, scale, p["mu"], p["w_r"], p["w_k"], p["w_v"], p["w_g"],
      p["w1"], p["w2"], p["a1"], p["a2"], p["w0"], p["a0"], p["k_k"], p["k_a"])


def _scan_body(lw_ref, kk_ref, b_ref, kd_ref, v_ref, r_ref, s0_ref, o_ref, s1_ref, h_ref, *, reverse, gl):
    c = pl.program_id(1)
    nc = pl.num_programs(1)
    cs_len, d = lw_ref.shape[1], lw_ref.shape[2]
    n = RWKV_HEAD
    assert cs_len == n, "block-diagonal packing assumes chunk == head size"
    hp = gl // n

    @pl.when(c == 0)
    def _():
        h_ref[...] = s0_ref[0]

    ri = lax.broadcasted_iota(jnp.int32, (cs_len, cs_len), 0)
    ci = lax.broadcasted_iota(jnp.int32, (cs_len, cs_len), 1)
    tri = jnp.where((ci >= ri) if reverse else (ci <= ri), 1.0, 0.0).astype(BF16)
    lw = lw_ref[0]
    lw_hi = lw.astype(BF16)
    rem = lw - lw_hi.astype(F32)
    lw_mid = rem.astype(BF16)
    lw_lo = (rem - lw_mid.astype(F32)).astype(BF16)
    cum = _dot(tri, lw_hi) + _dot(tri, lw_mid) + _dot(tri, lw_lo)
    tot = cum[0:1] if reverse else cum[cs_len - 1:cs_len]
    g_incl = jnp.exp(cum)
    g_excl = jnp.exp(cum - lw)
    g_inv = jnp.exp(-cum)
    g_tot = jnp.exp(tot)
    kkt = kk_ref[0] * g_excl
    kh = kd_ref[0] * g_inv
    bh = b_ref[0] * g_inv
    rt = r_ref[0] * g_incl
    kb = kh * g_tot
    bb = bh * g_tot
    vv = v_ref[0]

    bdmask = _iota_div((gl, gl), 0, n) == _iota_div((gl, gl), 1, n)
    i_s = lax.broadcasted_iota(jnp.int32, (cs_len, gl), 0)
    j_s = _iota_mod((cs_len, gl), 1, n)
    strict = (j_s > i_s) if reverse else (j_s < i_s)
    incl = (j_s >= i_s) if reverse else (j_s <= i_s)
    eye = j_s == i_s
    lane_head = _iota_div((n, gl), 1, n)

    def bd(y):
        return jnp.where(bdmask, jnp.concatenate([y] * hp, axis=0), 0.0).astype(BF16)

    def mm(a, bm):
        return _dot(a.astype(BF16), bm)

    def mm_split(a, w):
        a_hi = a.astype(BF16)
        a_lo = (a - a_hi.astype(F32)).astype(BF16)
        w_hi = w.astype(BF16).astype(F32)
        rows = a.shape[0]
        s = _dot(jnp.concatenate([a_hi, a_lo], axis=0), bd(w_hi))
        return s[:rows] + s[rows:] + _dot(a_hi, bd(w - w_hi))

    def mm_t(a, bm):
        return lax.dot_general(a.astype(BF16), bm, (((1,), (1,)), ((), ())), preferred_element_type=F32)

    def collapse(m):
        out = jnp.where(lane_head == 0, m[0:n], 0.0)
        for hh in range(1, hp):
            out = out + jnp.where(lane_head == hh, m[hh * n:(hh + 1) * n], 0.0)
        return out

    groups = [slice(g * gl, (g + 1) * gl) for g in range(d // gl)]
    cat0 = functools.partial(jnp.concatenate, axis=0)
    lhs = [cat0([kkt[:, sl], rt[:, sl]]) for sl in groups]
    ab = [mm_t(l, bd(bh[:, sl])) for l, sl in zip(lhs, groups)]
    ak = [mm_t(l, bd(kh[:, sl])) for l, sl in zip(lhs, groups)]
    a_m = [jnp.where(strict, y[:cs_len], 0.0) for y in ab]
    m2 = [jnp.where(incl, y[cs_len:], 0.0) for y in ab]
    b_m = [jnp.where(strict, y[:cs_len], 0.0) for y in ak]
    m1 = [jnp.where(incl, y[cs_len:], 0.0) for y in ak]
    xp = [-y for y in a_m]
    tm_ = [jnp.where(eye, 1.0, 0.0) + y for y in xp]
    xp = [mm_split(y, y) for y in xp]
    for _ in range(int(math.log2(cs_len)) - 2):
        st = [mm_split(cat0([y, z]), y) for y, z in zip(xp, tm_)]
        xp = [y[:cs_len] for y in st]
        tm_ = [z + y[cs_len:] for y, z in zip(st, tm_)]
    tm_ = [z + mm_split(z, y) for y, z in zip(xp, tm_)]
    wt = [mm(z, bd(kkt[:, sl])) for z, sl in zip(tm_, groups)]
    sv = [mm(cat0([y, z]), bd(vv[:, sl])) for y, z, sl in zip(b_m, m1, groups)]
    u0 = [mm(z, bd(y[:cs_len])) for y, z in zip(sv, tm_)]
    rbar = [rt[:, sl] - mm(y, bd(z)) for y, z, sl in zip(m2, wt, groups)]
    o0 = [y[cs_len:] - mm(z, bd(w)) for y, z, w in zip(sv, m2, u0)]
    bbt = [bb[:, sl].T for sl in groups]
    kbt = [kb[:, sl].T for sl in groups]
    pc = [mm(y, z.astype(BF16)) for y, z in zip(bbt, wt)]
    gc = [mm(jnp.concatenate([y, -z], axis=1), cat0([vv[:, sl], w]).astype(BF16))
          for y, z, w, sl in zip(kbt, bbt, u0, groups)]
    p_s = [jnp.where(eye, g_tot[:, sl], 0.0) - collapse(y) for y, sl in zip(pc, groups)]
    g_s = [collapse(y) for y in gc]
    so = [mm(cat0([y, z]), bd(h_ref[:, sl])) for y, z, sl in zip(rbar, p_s, groups)]
    for y, z, w, sl in zip(so, o0, g_s, groups):
        o_ref[0, :, sl] = y[:cs_len] + z
        h_ref[:, sl] = y[cs_len:] + w

    @pl.when(c == nc - 1)
    def _():
        s1_ref[0] = h_ref[...]


def _rwkv_scan(lw, kk, b, kd, v, r, state, *, reverse):
    bsz, t, d = lw.shape
    cs = SCAN_CHUNK
    nc = t // cs
    gl = min(d, 256)
    if reverse:
        row = pl.BlockSpec((1, cs, d), lambda bi, c: (bi, nc - 1 - c, 0))
    else:
        row = pl.BlockSpec((1, cs, d), lambda bi, c: (bi, c, 0))
    st = pl.BlockSpec((1, RWKV_HEAD, d), lambda bi, c: (bi, 0, 0))
    return pl.pallas_call(
        functools.partial(_scan_body, reverse=reverse, gl=gl),
        grid=(bsz, nc),
        in_specs=[row] * 6 + [st],
        out_specs=[row, st],
        out_shape=[jax.ShapeDtypeStruct((bsz, t, d), F32), jax.ShapeDtypeStruct((bsz, RWKV_HEAD, d), F32)],
        scratch_shapes=[pltpu.VMEM((RWKV_HEAD, d), F32)],
        compiler_params=_cparams("parallel", "arbitrary"),
        name="rwkv_scan_rev" if reverse else "rwkv_scan_fwd",
    )(lw, kk, b, kd, v, r, state)


def _rwkv_out_body(o0_ref, o1_ref, r_ref, kd0_ref, kd1_ref, v_ref, g_ref, x_ref, rk_ref, lg_ref, lb_ref, w_ref,
                   gate_ref, out_ref, y_ref):
    d = x_ref.shape[-1]
    ones = _head_ones(LANES, RWKV_HEAD)
    inv_n = 1.0 / RWKV_HEAD
    for c in range(d // LANES):
        sl = slice(c * LANES, (c + 1) * LANES)
        o = o0_ref[0, :, sl] + o1_ref[0, :, sl]
        mean = _dot_hi(o, ones) * inv_n
        cen = o - mean
        var = _dot_hi(cen * cen, ones) * inv_n
        on = cen * lax.rsqrt(var + GN_EPS) * lg_ref[:, sl] + lb_ref[:, sl]
        ksum = kd0_ref[0, :, sl] + kd1_ref[0, :, sl]
        bonus = _dot_hi(r_ref[0, :, sl] * ksum * rk_ref[:, sl], ones) * v_ref[0, :, sl]
        y_ref[:, sl] = ((on + bonus) * _silu(g_ref[0, :, sl])).astype(BF16)
    out_ref[0] = x_ref[0] + gate_ref[0] * _dot(y_ref[...], w_ref[...])


def _rwkv_out(o0, o1, r, kd0, kd1, v, g, x, r_k, ln_g, ln_b, w_o, gate, *, tm):
    bsz, t, d = x.shape
    row = pl.BlockSpec((1, tm, d), lambda b, i: (b, i, 0))
    return pl.pallas_call(
        _rwkv_out_body,
        grid=(bsz, t // tm),
        in_specs=[row] * 8 + [_const_spec((1, d))] * 3 + [_const_spec((d, d)), _mod_spec(gate)],
        out_specs=row,
        out_shape=jax.ShapeDtypeStruct((bsz, t, d), F32),
        scratch_shapes=[pltpu.VMEM((tm, d), BF16)],
        compiler_params=_cparams("parallel", "parallel"),
        name="rwkv_out",
    )(o0, o1, r, kd0, kd1, v, g, x, r_k, ln_g, ln_b, w_o, gate)


def _head_norm(x, g):
    ms = jnp.mean(x * x, axis=-1, keepdims=True)
    return x * lax.rsqrt(ms + NORM_EPS) * g


def _rope(x, cos, sin_signed):
    lane = _iota_mod(x.shape, 1, ATTN_HEAD // 2)
    swapped = jnp.where(lane < ATTN_HEAD // 4, pltpu.roll(x, ATTN_HEAD - ATTN_HEAD // 4, 1),
                        pltpu.roll(x, ATTN_HEAD // 4, 1))
    return x * cos + swapped * sin_signed


def _attn_in_body(x_ref, g_ref, sh_ref, sc_ref, w_ref, qg_ref, kg_ref, cos_ref, sin_ref, *outs,
                  qw, kvw, with_q, rope):
    h = _norm_mod(x_ref[0], g_ref[...], sh_ref[0], sc_ref[0]).astype(BF16)
    hd = ATTN_HEAD
    if with_q:
        q_ref, k_ref, v_ref, gt_ref = outs
        base_k = qw
    else:
        k_ref, v_ref = outs
        base_k = 0
    cos, sin = cos_ref[...], sin_ref[...]
    if with_q:
        for n in range(qw // hd):
            q = _head_norm(_dot(h, w_ref[:, n * hd:(n + 1) * hd]), qg_ref[...])
            q_ref[0, :, n * hd:(n + 1) * hd] = _rope(q, cos, sin).astype(BF16)
    for n in range(kvw // hd):
        k = _head_norm(_dot(h, w_ref[:, base_k + n * hd:base_k + (n + 1) * hd]), kg_ref[...])
        if rope:
            k = _rope(k, cos, sin)
        k_ref[0, :, n * hd:(n + 1) * hd] = k.astype(BF16)
    tn = 2 * hd
    for n in range(kvw // tn):
        v_ref[0, :, n * tn:(n + 1) * tn] = _dot(
            h, w_ref[:, base_k + kvw + n * tn:base_k + kvw + (n + 1) * tn]).astype(BF16)
    if with_q:
        for n in range(qw // tn):
            gt = _dot(h, w_ref[:, qw + 2 * kvw + n * tn:qw + 2 * kvw + (n + 1) * tn])
            gt_ref[0, :, n * tn:(n + 1) * tn] = _silu(gt)


def _attn_in(x, g, shift, scale, w, q_g, k_g, cos, sin, *, qw, kvw, with_q, rope, tm):
    bsz, t, d = x.shape
    row = pl.BlockSpec((1, tm, d), lambda b, i: (b, i, 0))
    tab = pl.BlockSpec((tm, ATTN_HEAD), lambda b, i: (i, 0))
    nw = w.shape[1]
    outs = []
    shapes = []
    if with_q:
        outs.append(pl.BlockSpec((1, tm, qw), lambda b, i: (b, i, 0)))
        shapes.append(jax.ShapeDtypeStruct((bsz, t, qw), BF16))
    for _ in range(2):
        outs.append(pl.BlockSpec((1, tm, kvw), lambda b, i: (b, i, 0)))
        shapes.append(jax.ShapeDtypeStruct((bsz, t, kvw), BF16))
    if with_q:
        outs.append(pl.BlockSpec((1, tm, qw), lambda b, i: (b, i, 0)))
        shapes.append(jax.ShapeDtypeStruct((bsz, t, qw), F32))
    return pl.pallas_call(
        functools.partial(_attn_in_body, qw=qw, kvw=kvw, with_q=with_q, rope=rope),
        grid=(bsz, t // tm),
        in_specs=[row, _const_spec((1, d)), _mod_spec(shift), _mod_spec(scale), _const_spec((d, nw)),
                  _const_spec((1, ATTN_HEAD)), _const_spec((1, ATTN_HEAD)), tab, tab],
        out_specs=outs,
        out_shape=shapes,
        compiler_params=_cparams("parallel", "parallel"),
        name="attn_in" if with_q else "attn_in_ctx",
    )(x, g, shift, scale, w, q_g, k_g, cos, sin)


def _flash_body(q_ref, kc_ref, vc_ref, k_ref, v_ref, o_ref):
    scale = ATTN_HEAD ** -0.5
    kc, vc, k, v = kc_ref[0], vc_ref[0], k_ref[0], v_ref[0]
    nt = (((1,), (1,)), ((), ()))
    for gq in range(ATTN_GROUP):
        sl = slice(gq * ATTN_HEAD, (gq + 1) * ATTN_HEAD)
        q = q_ref[0, :, sl]
        s_c = lax.dot_general(q, kc, nt, preferred_element_type=F32) * scale
        s_l = lax.dot_general(q, k, nt, preferred_element_type=F32) * scale
        m = jnp.maximum(jnp.max(s_c, axis=-1, keepdims=True), jnp.max(s_l, axis=-1, keepdims=True))
        p_c = jnp.exp(s_c - m)
        p_l = jnp.exp(s_l - m)
        den = jnp.sum(p_c, axis=-1, keepdims=True) + jnp.sum(p_l, axis=-1, keepdims=True)
        num = _dot(p_c.astype(BF16), vc) + _dot(p_l.astype(BF16), v)
        o_ref[0, :, sl] = num / den


def _flash(q, kc, vc, k, v, *, tq):
    bsz, t, qw = q.shape
    lc = kc.shape[1]
    hkv = k.shape[2] // ATTN_HEAD
    gw = ATTN_GROUP * ATTN_HEAD
    qspec = pl.BlockSpec((1, tq, gw), lambda b, hh, i: (b, i, hh))
    cspec = pl.BlockSpec((1, lc, ATTN_HEAD), lambda b, hh, i: (b, 0, hh))
    kspec = pl.BlockSpec((1, t, ATTN_HEAD), lambda b, hh, i: (b, 0, hh))
    return pl.pallas_call(
        _flash_body,
        grid=(bsz, hkv, t // tq),
        in_specs=[qspec, cspec, cspec, kspec, kspec],
        out_specs=qspec,
        out_shape=jax.ShapeDtypeStruct((bsz, t, qw), F32),
        compiler_params=_cparams("parallel", "parallel", "arbitrary"),
        name="flash_gqa",
    )(q, kc, vc, k, v)


def _gated_out_body(o_ref, sg_ref, x_ref, w_ref, gate_ref, out_ref):
    u = (o_ref[0] * sg_ref[0]).astype(BF16)
    out_ref[0] = x_ref[0] + gate_ref[0] * _dot(u, w_ref[...])


def _gated_out(o, sg, x, w, gate, *, tm):
    bsz, t, d = x.shape
    kdim = o.shape[-1]
    row = pl.BlockSpec((1, tm, d), lambda b, i: (b, i, 0))
    wide = pl.BlockSpec((1, tm, kdim), lambda b, i: (b, i, 0))
    return pl.pallas_call(
        _gated_out_body,
        grid=(bsz, t // tm),
        in_specs=[wide, wide, row, _const_spec((kdim, d)), _mod_spec(gate)],
        out_specs=row,
        out_shape=jax.ShapeDtypeStruct((bsz, t, d), F32),
        compiler_params=_cparams("parallel", "parallel"),
        name="attn_out",
    )(o, sg, x, w, gate)


def _final_norm_body(x_ref, g_ref, o_ref):
    x = x_ref[0]
    ms = jnp.mean(x * x, axis=-1, keepdims=True)
    o_ref[0] = x * lax.rsqrt(ms + NORM_EPS) * g_ref[...]


def _final_norm(x, g, *, tm):
    bsz, t, d = x.shape
    row = pl.BlockSpec((1, tm, d), lambda b, i: (b, i, 0))
    return pl.pallas_call(
        _final_norm_body,
        grid=(bsz, t // tm),
        in_specs=[row, _const_spec((1, d))],
        out_specs=row,
        out_shape=jax.ShapeDtypeStruct((bsz, t, d), F32),
        compiler_params=_cparams("parallel", "parallel"),
        name="final_norm",
    )(x, g)


def _rope_tables(t):
    axis_dim = ATTN_HEAD // 2
    rows = t // GRID_W
    row = jnp.repeat(jnp.arange(rows), GRID_W).astype(F32)
    col = (jnp.arange(rows * GRID_W) % GRID_W).astype(F32)
    inv = 1.0 / (ROPE_THETA ** (jnp.arange(0, axis_dim, 2, dtype=F32) / axis_dim))
    ar, ac = row[:, None] * inv, col[:, None] * inv
    cos = jnp.concatenate([jnp.cos(ar), jnp.cos(ar), jnp.cos(ac), jnp.cos(ac)], axis=-1)
    sin = jnp.concatenate([-jnp.sin(ar), jnp.sin(ar), -jnp.sin(ac), jnp.sin(ac)], axis=-1)
    return cos, sin


def _tile(t, want):
    tm = min(t, want)
    assert t % tm == 0
    return tm


def kernel(x, c, ctx, c_ctx, norm_g, mod_w, mod_b, conv_w_in, conv_dw, conv_db, conv_ln_g, conv_ln_b, conv_w_out, rwkv_mu, rwkv_w_r, rwkv_w_k, rwkv_w_v, rwkv_w_g, rwkv_w0, rwkv_w1, rwkv_w2, rwkv_a0, rwkv_a1, rwkv_a2, rwkv_k_k, rwkv_k_a, rwkv_r_k, rwkv_ln_g, rwkv_ln_b, rwkv_w_o, attn_w_in, attn_q_g, attn_k_g, attn_w_out, final_g):
    bsz, t, d = x.shape
    lc = ctx.shape[1]
    depth = mod_w.shape[0]
    qw = (d // 64) * ATTN_HEAD
    kvw = qw // ATTN_GROUP

    pad = (-(bsz + 1)) % 8
    cond = jnp.concatenate([c, c_ctx[None, :], jnp.zeros((pad, d), F32)], axis=0)
    mods = _modulation(cond, mod_w, mod_b)
    cos, sin = _rope_tables(t)
    final_row = final_g.reshape(1, d)

    xc = ctx
    for i in range(depth):
        kind, j = i % N_MIXERS, i // N_MIXERS
        ctx_out = any(l % N_MIXERS != 0 for l in range(i + 1, depth))
        ctx_in = ctx_out or kind != 0
        last = i == depth - 1
        m = mods[i]
        shift, scale, gate = (m[:bsz, None, k * d:(k + 1) * d] for k in range(3))
        shift_c, scale_c, gate_c = (m[bsz:bsz + 1, None, k * d:(k + 1) * d] for k in range(3))
        g = norm_g[i].reshape(1, d)
        if kind == 0:
            w3 = conv_w_in[j].reshape(d, 3, d).transpose(1, 0, 2).astype(BF16)
            w_out = conv_w_out[j].astype(BF16)
            cp = (conv_dw[j], conv_db[j].reshape(1, d), conv_ln_g[j].reshape(1, d), conv_ln_b[j].reshape(1, d), w_out)
            y, sg = _conv_in(x, g, shift, scale, w3, tm=_tile(t, 256))
            x_new = _conv_mid(y, sg, x, *cp, gate, final_row if last else None, tq=_tile(t, 256))
            if ctx_out:
                yc, sgc = _conv_in(xc, g, shift_c, scale_c, w3, tm=_tile(lc, 256))
                xc = _conv_mid(yc, sgc, xc, *cp, gate_c, None, tq=_tile(lc, 256))
            x = x_new
        elif kind == 1:
            lora = rwkv_w1.shape[-1]
            zero = jnp.zeros((lora, d), F32)

            def padded(w2):
                return jnp.stack([jnp.concatenate([w2[0], zero], 0), jnp.concatenate([zero, w2[1]], 0)]).astype(BF16)

            p = dict(
                mu=rwkv_mu[j], w_r=rwkv_w_r[j].astype(BF16), w_k=rwkv_w_k[j].astype(BF16),
                w_v=rwkv_w_v[j].astype(BF16), w_g=rwkv_w_g[j].astype(BF16),
                w1=jnp.concatenate([rwkv_w1[j, 0], rwkv_w1[j, 1]], axis=-1).astype(BF16), w2=padded(rwkv_w2[j]),
                a1=jnp.concatenate([rwkv_a1[j, 0], rwkv_a1[j, 1]], axis=-1).astype(BF16), a2=padded(rwkv_a2[j]),
                w0=rwkv_w0[j], a0=rwkv_a0[j], k_k=rwkv_k_k[j].reshape(1, d), k_a=rwkv_k_a[j].reshape(1, d))
            p["w1"] = p["w1"].reshape(d, 2 * lora)
            fl = _rwkv_feat(x, g, shift, scale, p, tm=_tile(t, 256))
            fc = _rwkv_feat(xc, g, shift_c, scale_c, p, tm=_tile(lc, 256))
            state0 = jnp.zeros((bsz, RWKV_HEAD, d), F32)
            o_l, o_c = [], []
            for dr, rev in ((0, False), (1, True)):
                oc_d, st = _rwkv_scan(fc[4 + dr], fc[3], fc[6 + dr], fc[8 + dr], fc[1], fc[0], state0, reverse=rev)
                ol_d, _ = _rwkv_scan(fl[4 + dr], fl[3], fl[6 + dr], fl[8 + dr], fl[1], fl[0], st, reverse=rev)
                o_l.append(ol_d)
                o_c.append(oc_d)
            ro = (rwkv_r_k[j].reshape(1, d), rwkv_ln_g[j].reshape(1, d), rwkv_ln_b[j].reshape(1, d),
                  rwkv_w_o[j].astype(BF16))
            x_new = _rwkv_out(o_l[0], o_l[1], fl[0], fl[8], fl[9], fl[1], fl[2], x, *ro, gate, tm=_tile(t, 256))
            if ctx_out:
                xc = _rwkv_out(o_c[0], o_c[1], fc[0], fc[8], fc[9], fc[1], fc[2], xc, *ro, gate_c, tm=_tile(lc, 256))
            x = x_new
        else:
            w_in = attn_w_in[j].astype(BF16)
            qg, kg = attn_q_g[j].reshape(1, ATTN_HEAD), attn_k_g[j].reshape(1, ATTN_HEAD)
            q, k, v, sg = _attn_in(x, g, shift, scale, w_in, qg, kg, cos, sin, qw=qw, kvw=kvw, with_q=True,
                                   rope=True, tm=_tile(t, 256))
            if ctx_out:
                raise NotImplementedError("context-stream attention output is not needed at this depth")
            no_rope = jnp.zeros((lc, ATTN_HEAD), F32)
            kc, vc = _attn_in(xc, g, shift_c, scale_c, w_in[:, qw:qw + 2 * kvw], qg, kg, no_rope, no_rope,
                              qw=qw, kvw=kvw, with_q=False, rope=False, tm=_tile(lc, 256))
            o = _flash(q, kc, vc, k, v, tq=_tile(t, 256))
            x = _gated_out(o, sg, x, attn_w_out[j].astype(BF16), gate, tm=_tile(t, 256))
        if last and kind != 0:
            x = _final_norm(x, final_row, tm=_tile(t, 256))
    return x
```

```python
import functools
import math

import jax
import jax.numpy as jnp
from jax import lax
from jax.experimental import pallas as pl
from jax.experimental.pallas import tpu as pltpu

N_MIXERS = 3
RWKV_HEAD = 64
ATTN_HEAD = 128
ATTN_GROUP = 2
GRID_W = 64
ROPE_THETA = 10000.0
NORM_EPS = 1e-6
LN_EPS = 1e-5
GN_EPS = 64e-5

V7X_VMEM_BYTES = 64 * 1024 * 1024
VMEM_LIMIT_BYTES = V7X_VMEM_BYTES - 8 * 1024 * 1024
LANES = 128
SUBLANES = 8
INV_BLOCK = 16
SCAN_CHUNK = 64
HALO = 16

F32 = jnp.float32
BF16 = jnp.bfloat16
HI = lax.Precision.HIGHEST


def _cparams(*sem):
    return pltpu.CompilerParams(dimension_semantics=sem, vmem_limit_bytes=VMEM_LIMIT_BYTES)


def _const_spec(shape):
    nd = len(shape)
    return pl.BlockSpec(shape, lambda *_: (0,) * nd, pipeline_mode=pl.Buffered(1))


def _mod_spec(arr):
    nb, _, d = arr.shape
    if nb == 1:
        return pl.BlockSpec((1, 1, d), lambda b, i: (0, 0, 0))
    return pl.BlockSpec((1, 1, d), lambda b, i: (b, 0, 0))


def _sigmoid(x):
    return 1.0 / (1.0 + jnp.exp(-x))


def _silu(x):
    return x * _sigmoid(x)


def _norm_mod(x, g, shift, scale):
    ms = jnp.mean(x * x, axis=-1, keepdims=True)
    n = x * lax.rsqrt(ms + NORM_EPS) * g
    return n * (1.0 + scale) + shift


def _dot(a, b):
    return jnp.dot(a, b, preferred_element_type=F32)


def _dot_hi(a, b):
    return jnp.dot(a, b, preferred_element_type=F32, precision=HI)


def _iota_div(shape, axis, n):
    assert n & (n - 1) == 0
    return lax.shift_right_logical(lax.broadcasted_iota(jnp.int32, shape, axis), n.bit_length() - 1)


def _iota_mod(shape, axis, n):
    assert n & (n - 1) == 0
    return lax.broadcasted_iota(jnp.int32, shape, axis) & (n - 1)


def _head_ones(width, head):
    r = _iota_div((width, width), 0, head)
    c = _iota_div((width, width), 1, head)
    return jnp.where(r == c, 1.0, 0.0).astype(BF16)


def _head_sum(x, ones):
    hi = x.astype(BF16)
    lo = (x - hi.astype(F32)).astype(BF16)
    rows = x.shape[0]
    s = _dot(jnp.concatenate([hi, lo], axis=0), ones)
    return s[:rows] + s[rows:]


def _modulation_body(c_ref, w_ref, b_ref, o_ref):
    s = _silu(c_ref[...])
    o_ref[0] = _dot_hi(s, w_ref[0]) + b_ref[0]


def _modulation(cond, mod_w, mod_b):
    depth, d, d3 = mod_w.shape
    rows = cond.shape[0]
    return pl.pallas_call(
        _modulation_body,
        grid=(depth, d3 // d),
        in_specs=[
            pl.BlockSpec((rows, d), lambda i, j: (0, 0)),
            pl.BlockSpec((1, d, d), lambda i, j: (i, 0, j)),
            pl.BlockSpec((1, 1, d), lambda i, j: (i, 0, j)),
        ],
        out_specs=pl.BlockSpec((1, rows, d), lambda i, j: (i, 0, j)),
        out_shape=jax.ShapeDtypeStruct((depth, rows, d3), F32),
        compiler_params=_cparams("parallel", "parallel"),
        name="modulation",
    )(cond, mod_w, mod_b.reshape(depth, 1, d3))


def _conv_in_body(x_ref, g_ref, sh_ref, sc_ref, w_ref, y_ref, sg_ref, *, tn):
    d = x_ref.shape[-1]
    h = _norm_mod(x_ref[0], g_ref[...], sh_ref[0], sc_ref[0]).astype(BF16)
    for n in range(d // tn):
        cols = slice(n * tn, (n + 1) * tn)
        a = _dot(h, w_ref[0, :, cols])
        b = _dot(h, w_ref[1, :, cols])
        gt = _dot(h, w_ref[2, :, cols])
        y_ref[0, :, cols] = a * _sigmoid(b)
        sg_ref[0, :, cols] = _silu(gt)


def _conv_in(x, g, shift, scale, w3, *, tm):
    bsz, t, d = x.shape
    tn = min(d, 256)
    row = pl.BlockSpec((1, tm, d), lambda b, i: (b, i, 0))
    return pl.pallas_call(
        functools.partial(_conv_in_body, tn=tn),
        grid=(bsz, t // tm),
        in_specs=[row, _const_spec((1, d)), _mod_spec(shift), _mod_spec(scale), _const_spec((3, d, d))],
        out_specs=[row, row],
        out_shape=[jax.ShapeDtypeStruct((bsz, t, d), F32)] * 2,
        compiler_params=_cparams("parallel", "parallel"),
        name="conv_in",
    )(x, g, shift, scale, w3)


def _conv_mid_body(yp_ref, yc_ref, yn_ref, sg_ref, x_ref, dw_ref, db_ref, lg_ref, lb_ref, w_ref, gate_ref,
                   *rest, width, final):
    if final:
        fg_ref, o_ref = rest
    else:
        (o_ref,) = rest
    i = pl.program_id(1)
    nt = pl.num_programs(1)
    tq = yc_ref.shape[1]
    prev = jnp.where(i > 0, yp_ref[0], 0.0)
    nxt = jnp.where(i < nt - 1, yn_ref[0], 0.0)
    win = jnp.concatenate([prev, yc_ref[0], nxt], axis=0)
    off = HALO - width // 2
    acc = db_ref[...]
    for res in range(SUBLANES):
        part = None
        for k in range(width):
            if (off + k) % SUBLANES != res:
                continue
            base = off + k - res
            term = win[base:base + tq + SUBLANES] * dw_ref[k:k + 1, :]
            part = term if part is None else part + term
        if part is not None:
            acc = acc + part[res:res + tq]
    mean = jnp.mean(acc, axis=-1, keepdims=True)
    cen = acc - mean
    var = jnp.mean(cen * cen, axis=-1, keepdims=True)
    yn = cen * lax.rsqrt(var + LN_EPS) * lg_ref[...] + lb_ref[...]
    u = (_silu(yn) * sg_ref[0]).astype(BF16)
    out = x_ref[0] + gate_ref[0] * _dot(u, w_ref[...])
    if final:
        ms = jnp.mean(out * out, axis=-1, keepdims=True)
        out = out * lax.rsqrt(ms + NORM_EPS) * fg_ref[...]
    o_ref[0] = out


def _conv_mid(y, sg, x, dw, db, ln_g, ln_b, w_out, gate, final_g, *, tq):
    bsz, t, d = x.shape
    width = dw.shape[0]
    assert width // 2 <= HALO and t % HALO == 0 and tq % HALO == 0
    r = tq // HALO
    nh = t // HALO
    row = pl.BlockSpec((1, tq, d), lambda b, i: (b, i, 0))
    prev = pl.BlockSpec((1, HALO, d), lambda b, i: (b, jnp.maximum(i * r - 1, 0), 0))
    nxt = pl.BlockSpec((1, HALO, d), lambda b, i: (b, jnp.minimum((i + 1) * r, nh - 1), 0))
    final = final_g is not None
    in_specs = [prev, row, nxt, row, row, _const_spec((width, d)), _const_spec((1, d)), _const_spec((1, d)),
                _const_spec((1, d)), _const_spec((d, d)), _mod_spec(gate)]
    args = [y, y, y, sg, x, dw, db, ln_g, ln_b, w_out, gate]
    if final:
        in_specs.append(_const_spec((1, d)))
        args.append(final_g)
    return pl.pallas_call(
        functools.partial(_conv_mid_body, width=width, final=final),
        grid=(bsz, t // tq),
        in_specs=in_specs,
        out_specs=row,
        out_shape=jax.ShapeDtypeStruct((bsz, t, d), F32),
        compiler_params=_cparams("parallel", "parallel"),
        name="conv_mid",
    )(*args)


def _rwkv_feat_body(xp_ref, xc_ref, xn_ref, g_ref, sh_ref, sc_ref, mu_ref, wr_ref, wk_ref, wv_ref, wg_ref,
                    w1_ref, w2_ref, a1_ref, a2_ref, w0_ref, a0_ref, kk_ref_in, ka_ref,
                    r_ref, v_ref, g_out_ref, kk_ref, lw0_ref, lw1_ref, b0_ref, b1_ref, kd0_ref, kd1_ref):
    i = pl.program_id(1)
    nt = pl.num_programs(1)
    tm, d = xc_ref.shape[1], xc_ref.shape[2]
    g, sh, sc = g_ref[...], sh_ref[0], sc_ref[0]
    h = _norm_mod(xc_ref[0], g, sh, sc)
    hp = _norm_mod(xp_ref[0], g, sh, sc)[7:8]
    hn = _norm_mod(xn_ref[0], g, sh, sc)[0:1]
    hp = jnp.where(i > 0, hp, 0.0)
    hn = jnp.where(i < nt - 1, hn, 0.0)
    rows = lax.broadcasted_iota(jnp.int32, (tm, d), 0)
    hm1 = jnp.where(rows == 0, hp, pltpu.roll(h, 1, 0))
    hp1 = jnp.where(rows == tm - 1, hn, pltpu.roll(h, tm - 1, 0))
    xx = 0.5 * (hm1 + hp1) - h

    def lerp(n):
        return (h + xx * mu_ref[n:n + 1, :]).astype(BF16)

    r = _dot(lerp(0), wr_ref[...])
    k = _dot(lerp(2), wk_ref[...])
    v = _dot(lerp(3), wv_ref[...])
    r_ref[0] = r
    v_ref[0] = v
    g_out_ref[0] = _dot(lerp(5), wg_ref[...])
    kx = k * kk_ref_in[...]
    ones = _head_ones(LANES, RWKV_HEAD)
    parts = []
    for c in range(d // LANES):
        sl = slice(c * LANES, (c + 1) * LANES)
        q = kx[:, sl]
        parts.append(q * lax.rsqrt(_head_sum(q * q, ones) + 1e-12))
    kk = parts[0] if len(parts) == 1 else jnp.concatenate(parts, axis=1)
    kk_ref[0] = kk
    tw = jnp.tanh(_dot(lerp(1), w1_ref[...])).astype(BF16)
    ta = _dot(lerp(4), a1_ref[...]).astype(BF16)
    for dr, (lw_ref, b_ref, kd_ref) in enumerate(((lw0_ref, b0_ref, kd0_ref), (lw1_ref, b1_ref, kd1_ref))):
        z = w0_ref[dr:dr + 1, :] + _dot(tw, w2_ref[dr])
        sp = jnp.maximum(-z, 0.0) + jnp.log(1.0 + jnp.exp(-jnp.abs(z)))
        lw_ref[0] = -jnp.exp(-sp - 0.5)
        a = _sigmoid(a0_ref[dr:dr + 1, :] + _dot(ta, a2_ref[dr]))
        b_ref[0] = kk * a
        kd_ref[0] = k * (1.0 + (a - 1.0) * ka_ref[...])


def _rwkv_feat(x, g, shift, scale, p, *, tm):
    bsz, t, d = x.shape
    assert t % 8 == 0 and tm % 8 == 0
    r8 = tm // 8
    n8 = t // 8
    row = pl.BlockSpec((1, tm, d), lambda b, i: (b, i, 0))
    prev = pl.BlockSpec((1, 8, d), lambda b, i: (b, jnp.maximum(i * r8 - 1, 0), 0))
    nxt = pl.BlockSpec((1, 8, d), lambda b, i: (b, jnp.minimum((i + 1) * r8, n8 - 1), 0))
    lora = p["w1"].shape[-1]
    in_specs = [prev, row, nxt, _const_spec((1, d)), _mod_spec(shift), _mod_spec(scale), _const_spec((6, d)),
                _const_spec((d, d)), _const_spec((d, d)), _const_spec((d, d)), _const_spec((d, d)),
                _const_spec((d, lora)), _const_spec((2, lora, d)), _const_spec((d, lora)), _const_spec((2, lora, d)),
                _const_spec((2, d)), _const_spec((2, d)), _const_spec((1, d)), _const_spec((1, d))]
    return pl.pallas_call(
        _rwkv_feat_body,
        grid=(bsz, t // tm),
        in_specs=in_specs,
        out_specs=[row] * 10,
        out_shape=[jax.ShapeDtypeStruct((bsz, t, d), F32)] * 10,
        compiler_params=_cparams("parallel", "parallel"),
        name="rwkv_feat",
    )(x, x, x, g, shift, scale, p["mu"], p["w_r"], p["w_k"], p["w_v"], p["w_g"],
      p["w1"], p["w2"], p["a1"], p["a2"], p["w0"], p["a0"], p["k_k"], p["k_a"])


def _scan_body(lw_ref, kk_ref, b_ref, kd_ref, v_ref, r_ref, s0_ref, o_ref, s1_ref, h_ref, *, reverse, gl):
    c = pl.program_id(1)
    nc = pl.num_programs(1)
    nb, cs_len, d = lw_ref.shape
    n = RWKV_HEAD
    assert cs_len == n, "block-diagonal packing assumes chunk == head size"
    hp = gl // n

    @pl.when(c == 0)
    def _():
        h_ref[...] = s0_ref[...]

    ri = lax.broadcasted_iota(jnp.int32, (cs_len, cs_len), 0)
    ci = lax.broadcasted_iota(jnp.int32, (cs_len, cs_len), 1)
    tri = jnp.where((ci >= ri) if reverse else (ci <= ri), 1.0, 0.0).astype(BF16)
    kkt, kh, bh, rt, kb, bb, vv, g_tot = ([] for _ in range(8))
    for bi in range(nb):
        lw = lw_ref[bi]
        lw_hi = lw.astype(BF16)
        rem = lw - lw_hi.astype(F32)
        lw_mid = rem.astype(BF16)
        lw_lo = (rem - lw_mid.astype(F32)).astype(BF16)
        cum = _dot(tri, lw_hi) + _dot(tri, lw_mid) + _dot(tri, lw_lo)
        tot = cum[0:1] if reverse else cum[cs_len - 1:cs_len]
        g_inv = jnp.exp(-cum)
        g_tot.append(jnp.exp(tot))
        kkt.append(kk_ref[bi] * jnp.exp(cum - lw))
        kh.append(kd_ref[bi] * g_inv)
        bh.append(b_ref[bi] * g_inv)
        rt.append(r_ref[bi] * jnp.exp(cum))
        kb.append(kh[bi] * g_tot[bi])
        bb.append(bh[bi] * g_tot[bi])
        vv.append(v_ref[bi])

    bdmask = _iota_div((gl, gl), 0, n) == _iota_div((gl, gl), 1, n)
    i_s = lax.broadcasted_iota(jnp.int32, (cs_len, gl), 0)
    j_s = _iota_mod((cs_len, gl), 1, n)
    strict = (j_s > i_s) if reverse else (j_s < i_s)
    incl = (j_s >= i_s) if reverse else (j_s <= i_s)
    eye = j_s == i_s
    shift = INV_BLOCK.bit_length() - 1
    dblock = lax.shift_right_logical(i_s, shift) == lax.shift_right_logical(j_s, shift)
    lane_head = _iota_div((n, gl), 1, n)

    def bd(y):
        return jnp.where(bdmask, jnp.concatenate([y] * hp, axis=0), 0.0).astype(BF16)

    def mm(a, bm):
        return _dot(a.astype(BF16), bm)

    def mm_t(a, bm):
        return lax.dot_general(a.astype(BF16), bm, (((1,), (1,)), ((), ())), preferred_element_type=F32)

    def collapse(m):
        out = jnp.where(lane_head == 0, m[0:n], 0.0)
        for hh in range(1, hp):
            out = out + jnp.where(lane_head == hh, m[hh * n:(hh + 1) * n], 0.0)
        return out

    groups = [(bi, slice(g * gl, (g + 1) * gl)) for bi in range(nb) for g in range(d // gl)]
    cat0 = functools.partial(jnp.concatenate, axis=0)
    lhs = [cat0([kkt[bi][:, sl], rt[bi][:, sl]]) for bi, sl in groups]
    ab = [mm_t(l, bd(bh[bi][:, sl])) for l, (bi, sl) in zip(lhs, groups)]
    ak = [mm_t(l, bd(kh[bi][:, sl])) for l, (bi, sl) in zip(lhs, groups)]
    a_m = [jnp.where(strict, y[:cs_len], 0.0) for y in ab]
    m2 = [jnp.where(incl, y[cs_len:], 0.0) for y in ab]
    b_m = [jnp.where(strict, y[:cs_len], 0.0) for y in ak]
    m1 = [jnp.where(incl, y[cs_len:], 0.0) for y in ak]
    ident = jnp.where(eye, 1.0, 0.0)
    a_d = [jnp.where(dblock, y, 0.0) for y in a_m]
    xp = [-y for y in a_d]
    tm_ = [ident + y for y in xp]
    xp = [mm(y, bd(y)) for y in xp]
    for _ in range(int(math.log2(INV_BLOCK)) - 2):
        st = [mm(cat0([y, z]), bd(y)) for y, z in zip(xp, tm_)]
        xp = [y[:cs_len] for y in st]
        tm_ = [z + y[cs_len:] for y, z in zip(st, tm_)]
    t_d = [z + mm(z, bd(y)) for y, z in zip(xp, tm_)]
    nn = [mm(z, bd(y - w)) for z, y, w in zip(t_d, a_m, a_d)]
    vm = [ident - y for y in nn]
    for _ in range(int(math.log2(cs_len // INV_BLOCK)) - 1):
        nn = [mm(y, bd(y)) for y in nn]
        vm = [z + mm(z, bd(y)) for y, z in zip(nn, vm)]
    tm_ = [mm(z, bd(y)) for y, z in zip(t_d, vm)]
    wt = [mm(z, bd(kkt[bi][:, sl])) for z, (bi, sl) in zip(tm_, groups)]
    sv = [mm(cat0([y, z]), bd(vv[bi][:, sl])) for y, z, (bi, sl) in zip(b_m, m1, groups)]
    u0 = [mm(z, bd(y[:cs_len])) for y, z in zip(sv, tm_)]
    rbar = [rt[bi][:, sl] - mm(y, bd(z)) for y, z, (bi, sl) in zip(m2, wt, groups)]
    o0 = [y[cs_len:] - mm(z, bd(w)) for y, z, w in zip(sv, m2, u0)]
    bbt = [bb[bi][:, sl].T for bi, sl in groups]
    kbt = [kb[bi][:, sl].T for bi, sl in groups]
    pc = [mm(y, z.astype(BF16)) for y, z in zip(bbt, wt)]
    gc = [mm(jnp.concatenate([y, -z], axis=1), cat0([vv[bi][:, sl], w]).astype(BF16))
          for y, z, w, (bi, sl) in zip(kbt, bbt, u0, groups)]
    p_s = [jnp.where(eye, g_tot[bi][:, sl], 0.0) - collapse(y) for y, (bi, sl) in zip(pc, groups)]
    g_s = [collapse(y) for y in gc]
    so = [mm(cat0([y, z]), bd(h_ref[bi, :, sl])) for y, z, (bi, sl) in zip(rbar, p_s, groups)]
    for y, z, w, (bi, sl) in zip(so, o0, g_s, groups):
        o_ref[bi, :, sl] = y[:cs_len] + z
        h_ref[bi, :, sl] = y[cs_len:] + w

    @pl.when(c == nc - 1)
    def _():
        s1_ref[...] = h_ref[...]


def _rwkv_scan(lw, kk, b, kd, v, r, state, *, reverse):
    bsz, t, d = lw.shape
    cs = SCAN_CHUNK
    nc = t // cs
    gl = min(d, 256)
    nb = 2 if bsz % 2 == 0 else 1
    if reverse:
        row = pl.BlockSpec((nb, cs, d), lambda bi, c: (bi, nc - 1 - c, 0))
    else:
        row = pl.BlockSpec((nb, cs, d), lambda bi, c: (bi, c, 0))
    st = pl.BlockSpec((nb, RWKV_HEAD, d), lambda bi, c: (bi, 0, 0))
    return pl.pallas_call(
        functools.partial(_scan_body, reverse=reverse, gl=gl),
        grid=(bsz // nb, nc),
        in_specs=[row] * 6 + [st],
        out_specs=[row, st],
        out_shape=[jax.ShapeDtypeStruct((bsz, t, d), F32), jax.ShapeDtypeStruct((bsz, RWKV_HEAD, d), F32)],
        scratch_shapes=[pltpu.VMEM((nb, RWKV_HEAD, d), F32)],
        compiler_params=_cparams("parallel", "arbitrary"),
        name="rwkv_scan_rev" if reverse else "rwkv_scan_fwd",
    )(lw, kk, b, kd, v, r, state)


def _rwkv_out_body(o0_ref, o1_ref, r_ref, kd0_ref, kd1_ref, v_ref, g_ref, x_ref, rk_ref, lg_ref, lb_ref, w_ref,
                   gate_ref, out_ref, y_ref):
    d = x_ref.shape[-1]
    ones = _head_ones(LANES, RWKV_HEAD)
    inv_n = 1.0 / RWKV_HEAD
    for c in range(d // LANES):
        sl = slice(c * LANES, (c + 1) * LANES)
        o = o0_ref[0, :, sl] + o1_ref[0, :, sl]
        mean = _head_sum(o, ones) * inv_n
        cen = o - mean
        var = _head_sum(cen * cen, ones) * inv_n
        on = cen * lax.rsqrt(var + GN_EPS) * lg_ref[:, sl] + lb_ref[:, sl]
        ksum = kd0_ref[0, :, sl] + kd1_ref[0, :, sl]
        bonus = _head_sum(r_ref[0, :, sl] * ksum * rk_ref[:, sl], ones) * v_ref[0, :, sl]
        y_ref[:, sl] = ((on + bonus) * _silu(g_ref[0, :, sl])).astype(BF16)
    out_ref[0] = x_ref[0] + gate_ref[0] * _dot(y_ref[...], w_ref[...])


def _rwkv_out(o0, o1, r, kd0, kd1, v, g, x, r_k, ln_g, ln_b, w_o, gate, *, tm):
    bsz, t, d = x.shape
    row = pl.BlockSpec((1, tm, d), lambda b, i: (b, i, 0))
    return pl.pallas_call(
        _rwkv_out_body,
        grid=(bsz, t // tm),
        in_specs=[row] * 8 + [_const_spec((1, d))] * 3 + [_const_spec((d, d)), _mod_spec(gate)],
        out_specs=row,
        out_shape=jax.ShapeDtypeStruct((bsz, t, d), F32),
        scratch_shapes=[pltpu.VMEM((tm, d), BF16)],
        compiler_params=_cparams("parallel", "parallel"),
        name="rwkv_out",
    )(o0, o1, r, kd0, kd1, v, g, x, r_k, ln_g, ln_b, w_o, gate)


def _head_norm(x, g):
    ms = jnp.mean(x * x, axis=-1, keepdims=True)
    return x * lax.rsqrt(ms + NORM_EPS) * g


def _rope(x, cos, sin_signed):
    lane = _iota_mod(x.shape, 1, ATTN_HEAD // 2)
    swapped = jnp.where(lane < ATTN_HEAD // 4, pltpu.roll(x, ATTN_HEAD - ATTN_HEAD // 4, 1),
                        pltpu.roll(x, ATTN_HEAD // 4, 1))
    return x * cos + swapped * sin_signed


def _attn_in_body(x_ref, g_ref, sh_ref, sc_ref, w_ref, qg_ref, kg_ref, cos_ref, sin_ref, *outs,
                  qw, kvw, with_q, rope):
    h = _norm_mod(x_ref[0], g_ref[...], sh_ref[0], sc_ref[0]).astype(BF16)
    hd = ATTN_HEAD
    if with_q:
        q_ref, k_ref, v_ref, gt_ref = outs
        base_k = qw
    else:
        k_ref, v_ref = outs
        base_k = 0
    cos, sin = cos_ref[...], sin_ref[...]
    tn = 2 * hd
    if with_q:
        for n in range(qw // tn):
            qq = _dot(h, w_ref[:, n * tn:(n + 1) * tn])
            for half in range(2):
                q = _head_norm(qq[:, half * hd:(half + 1) * hd], qg_ref[...])
                q_ref[0, :, n * tn + half * hd:n * tn + (half + 1) * hd] = (
                    _rope(q, cos, sin) * (hd ** -0.5)).astype(BF16)
    for n in range(kvw // tn):
        kk2 = _dot(h, w_ref[:, base_k + n * tn:base_k + (n + 1) * tn])
        for half in range(2):
            k = _head_norm(kk2[:, half * hd:(half + 1) * hd], kg_ref[...])
            if rope:
                k = _rope(k, cos, sin)
            k_ref[0, :, n * tn + half * hd:n * tn + (half + 1) * hd] = k.astype(BF16)
    for n in range(kvw // tn):
        v_ref[0, :, n * tn:(n + 1) * tn] = _dot(
            h, w_ref[:, base_k + kvw + n * tn:base_k + kvw + (n + 1) * tn]).astype(BF16)
    if with_q:
        for n in range(qw // tn):
            gt = _dot(h, w_ref[:, qw + 2 * kvw + n * tn:qw + 2 * kvw + (n + 1) * tn])
            gt_ref[0, :, n * tn:(n + 1) * tn] = _silu(gt)


def _attn_in(x, g, shift, scale, w, q_g, k_g, cos, sin, *, qw, kvw, with_q, rope, tm):
    bsz, t, d = x.shape
    row = pl.BlockSpec((1, tm, d), lambda b, i: (b, i, 0))
    tab = pl.BlockSpec((tm, ATTN_HEAD), lambda b, i: (i, 0))
    nw = w.shape[1]
    outs = []
    shapes = []
    if with_q:
        outs.append(pl.BlockSpec((1, tm, qw), lambda b, i: (b, i, 0)))
        shapes.append(jax.ShapeDtypeStruct((bsz, t, qw), BF16))
    for _ in range(2):
        outs.append(pl.BlockSpec((1, tm, kvw), lambda b, i: (b, i, 0)))
        shapes.append(jax.ShapeDtypeStruct((bsz, t, kvw), BF16))
    if with_q:
        outs.append(pl.BlockSpec((1, tm, qw), lambda b, i: (b, i, 0)))
        shapes.append(jax.ShapeDtypeStruct((bsz, t, qw), F32))
    return pl.pallas_call(
        functools.partial(_attn_in_body, qw=qw, kvw=kvw, with_q=with_q, rope=rope),
        grid=(bsz, t // tm),
        in_specs=[row, _const_spec((1, d)), _mod_spec(shift), _mod_spec(scale), _const_spec((d, nw)),
                  _const_spec((1, ATTN_HEAD)), _const_spec((1, ATTN_HEAD)), tab, tab],
        out_specs=outs,
        out_shape=shapes,
        compiler_params=_cparams("parallel", "parallel"),
        name="attn_in" if with_q else "attn_in_ctx",
    )(x, g, shift, scale, w, q_g, k_g, cos, sin)


def _flash_body(q_ref, k_ref, vt_ref, o_ref, *, sub):
    k, vt = k_ref[0], vt_ref[0]
    tq = q_ref.shape[1]
    units = [(slice(r, r + sub), slice(gq * ATTN_HEAD, (gq + 1) * ATTN_HEAD))
             for r in range(0, tq, sub) for gq in range(ATTN_GROUP)]
    nt = (((1,), (1,)), ((), ()))
    st = [lax.dot_general(k, q_ref[0, rs, sl], nt, preferred_element_type=F32) for rs, sl in units]
    m = [jnp.max(s, axis=0, keepdims=True) for s in st]
    p = [jnp.exp(s - mx) for s, mx in zip(st, m)]
    den = [jnp.sum(y, axis=0, keepdims=True) for y in p]
    ot = [_dot(vt, y.astype(BF16)) for y in p]
    for (rs, sl), y, dn in zip(units, ot, den):
        o_ref[0, rs, sl] = (y / dn).T


def _flash(q, k_all, vt_all, *, tq):
    bsz, t, qw = q.shape
    keys = k_all.shape[1]
    hkv = k_all.shape[2] // ATTN_HEAD
    gw = ATTN_GROUP * ATTN_HEAD
    qspec = pl.BlockSpec((1, tq, gw), lambda b, hh, i: (b, i, hh))
    kspec = pl.BlockSpec((1, keys, ATTN_HEAD), lambda b, hh, i: (b, 0, hh))
    vspec = pl.BlockSpec((1, ATTN_HEAD, keys), lambda b, hh, i: (b, hh, 0))
    return pl.pallas_call(
        functools.partial(_flash_body, sub=min(tq, 256)),
        grid=(bsz, hkv, t // tq),
        in_specs=[qspec, kspec, vspec],
        out_specs=qspec,
        out_shape=jax.ShapeDtypeStruct((bsz, t, qw), F32),
        compiler_params=_cparams("parallel", "parallel", "arbitrary"),
        name="flash_gqa",
    )(q, k_all, vt_all)


def _gated_out_body(o_ref, sg_ref, x_ref, w_ref, gate_ref, out_ref):
    u = (o_ref[0] * sg_ref[0]).astype(BF16)
    out_ref[0] = x_ref[0] + gate_ref[0] * _dot(u, w_ref[...])


def _gated_out(o, sg, x, w, gate, *, tm):
    bsz, t, d = x.shape
    kdim = o.shape[-1]
    row = pl.BlockSpec((1, tm, d), lambda b, i: (b, i, 0))
    wide = pl.BlockSpec((1, tm, kdim), lambda b, i: (b, i, 0))
    return pl.pallas_call(
        _gated_out_body,
        grid=(bsz, t // tm),
        in_specs=[wide, wide, row, _const_spec((kdim, d)), _mod_spec(gate)],
        out_specs=row,
        out_shape=jax.ShapeDtypeStruct((bsz, t, d), F32),
        compiler_params=_cparams("parallel", "parallel"),
        name="attn_out",
    )(o, sg, x, w, gate)


def _final_norm_body(x_ref, g_ref, o_ref):
    x = x_ref[0]
    ms = jnp.mean(x * x, axis=-1, keepdims=True)
    o_ref[0] = x * lax.rsqrt(ms + NORM_EPS) * g_ref[...]


def _final_norm(x, g, *, tm):
    bsz, t, d = x.shape
    row = pl.BlockSpec((1, tm, d), lambda b, i: (b, i, 0))
    return pl.pallas_call(
        _final_norm_body,
        grid=(bsz, t // tm),
        in_specs=[row, _const_spec((1, d))],
        out_specs=row,
        out_shape=jax.ShapeDtypeStruct((bsz, t, d), F32),
        compiler_params=_cparams("parallel", "parallel"),
        name="final_norm",
    )(x, g)


def _rope_tables(t):
    axis_dim = ATTN_HEAD // 2
    rows = t // GRID_W
    row = jnp.repeat(jnp.arange(rows), GRID_W).astype(F32)
    col = (jnp.arange(rows * GRID_W) % GRID_W).astype(F32)
    inv = 1.0 / (ROPE_THETA ** (jnp.arange(0, axis_dim, 2, dtype=F32) / axis_dim))
    ar, ac = row[:, None] * inv, col[:, None] * inv
    cos = jnp.concatenate([jnp.cos(ar), jnp.cos(ar), jnp.cos(ac), jnp.cos(ac)], axis=-1)
    sin = jnp.concatenate([-jnp.sin(ar), jnp.sin(ar), -jnp.sin(ac), jnp.sin(ac)], axis=-1)
    return cos, sin


def _tile(t, want):
    tm = min(t, want)
    assert t % tm == 0
    return tm


def kernel(x, c, ctx, c_ctx, norm_g, mod_w, mod_b, conv_w_in, conv_dw, conv_db, conv_ln_g, conv_ln_b, conv_w_out, rwkv_mu, rwkv_w_r, rwkv_w_k, rwkv_w_v, rwkv_w_g, rwkv_w0, rwkv_w1, rwkv_w2, rwkv_a0, rwkv_a1, rwkv_a2, rwkv_k_k, rwkv_k_a, rwkv_r_k, rwkv_ln_g, rwkv_ln_b, rwkv_w_o, attn_w_in, attn_q_g, attn_k_g, attn_w_out, final_g):
    bsz, t, d = x.shape
    lc = ctx.shape[1]
    depth = mod_w.shape[0]
    qw = (d // 64) * ATTN_HEAD
    kvw = qw // ATTN_GROUP

    pad = (-(bsz + 1)) % 8
    cond = jnp.concatenate([c, c_ctx[None, :], jnp.zeros((pad, d), F32)], axis=0)
    mods = _modulation(cond, mod_w, mod_b)
    cos, sin = _rope_tables(t)
    final_row = final_g.reshape(1, d)

    xc = ctx
    for i in range(depth):
        kind, j = i % N_MIXERS, i // N_MIXERS
        ctx_out = any(l % N_MIXERS != 0 for l in range(i + 1, depth))
        ctx_in = ctx_out or kind != 0
        last = i == depth - 1
        m = mods[i]
        shift, scale, gate = (m[:bsz, None, k * d:(k + 1) * d] for k in range(3))
        shift_c, scale_c, gate_c = (m[bsz:bsz + 1, None, k * d:(k + 1) * d] for k in range(3))
        g = norm_g[i].reshape(1, d)
        if kind == 0:
            w3 = conv_w_in[j].reshape(d, 3, d).transpose(1, 0, 2).astype(BF16)
            w_out = conv_w_out[j].astype(BF16)
            cp = (conv_dw[j], conv_db[j].reshape(1, d), conv_ln_g[j].reshape(1, d), conv_ln_b[j].reshape(1, d), w_out)
            y, sg = _conv_in(x, g, shift, scale, w3, tm=_tile(t, 256))
            x_new = _conv_mid(y, sg, x, *cp, gate, final_row if last else None, tq=_tile(t, 256))
            if ctx_out:
                yc, sgc = _conv_in(xc, g, shift_c, scale_c, w3, tm=_tile(lc, 256))
                xc = _conv_mid(yc, sgc, xc, *cp, gate_c, None, tq=_tile(lc, 256))
            x = x_new
        elif kind == 1:
            lora = rwkv_w1.shape[-1]
            zero = jnp.zeros((lora, d), F32)

            def padded(w2):
                return jnp.stack([jnp.concatenate([w2[0], zero], 0), jnp.concatenate([zero, w2[1]], 0)]).astype(BF16)

            p = dict(
                mu=rwkv_mu[j], w_r=rwkv_w_r[j].astype(BF16), w_k=rwkv_w_k[j].astype(BF16),
                w_v=rwkv_w_v[j].astype(BF16), w_g=rwkv_w_g[j].astype(BF16),
                w1=jnp.concatenate([rwkv_w1[j, 0], rwkv_w1[j, 1]], axis=-1).astype(BF16), w2=padded(rwkv_w2[j]),
                a1=jnp.concatenate([rwkv_a1[j, 0], rwkv_a1[j, 1]], axis=-1).astype(BF16), a2=padded(rwkv_a2[j]),
                w0=rwkv_w0[j], a0=rwkv_a0[j], k_k=rwkv_k_k[j].reshape(1, d), k_a=rwkv_k_a[j].reshape(1, d))
            p["w1"] = p["w1"].reshape(d, 2 * lora)
            fl = _rwkv_feat(x, g, shift, scale, p, tm=_tile(t, 256))
            fc = _rwkv_feat(xc, g, shift_c, scale_c, p, tm=_tile(lc, 256))
            state0 = jnp.zeros((bsz, RWKV_HEAD, d), F32)
            o_l, o_c = [], []
            for dr, rev in ((0, False), (1, True)):
                oc_d, st = _rwkv_scan(fc[4 + dr], fc[3], fc[6 + dr], fc[8 + dr], fc[1], fc[0], state0, reverse=rev)
                ol_d, _ = _rwkv_scan(fl[4 + dr], fl[3], fl[6 + dr], fl[8 + dr], fl[1], fl[0], st, reverse=rev)
                o_l.append(ol_d)
                o_c.append(oc_d)
            ro = (rwkv_r_k[j].reshape(1, d), rwkv_ln_g[j].reshape(1, d), rwkv_ln_b[j].reshape(1, d),
                  rwkv_w_o[j].astype(BF16))
            x_new = _rwkv_out(o_l[0], o_l[1], fl[0], fl[8], fl[9], fl[1], fl[2], x, *ro, gate, tm=_tile(t, 256))
            if ctx_out:
                xc = _rwkv_out(o_c[0], o_c[1], fc[0], fc[8], fc[9], fc[1], fc[2], xc, *ro, gate_c, tm=_tile(lc, 256))
            x = x_new
        else:
            w_in = attn_w_in[j].astype(BF16)
            qg, kg = attn_q_g[j].reshape(1, ATTN_HEAD), attn_k_g[j].reshape(1, ATTN_HEAD)
            q, k, v, sg = _attn_in(x, g, shift, scale, w_in, qg, kg, cos, sin, qw=qw, kvw=kvw, with_q=True,
                                   rope=True, tm=_tile(t, 256))
            if ctx_out:
                raise NotImplementedError("context-stream attention output is not needed at this depth")
            no_rope = jnp.zeros((lc, ATTN_HEAD), F32)
            kc, vc = _attn_in(xc, g, shift_c, scale_c, w_in[:, qw:qw + 2 * kvw], qg, kg, no_rope, no_rope,
                              qw=qw, kvw=kvw, with_q=False, rope=False, tm=_tile(lc, 256))
            k_all = jnp.concatenate([kc, k], axis=1)
            vt_all = jnp.concatenate([vc, v], axis=1).transpose(0, 2, 1)
            o = _flash(q, k_all, vt_all, tq=_tile(t, 512))
            x = _gated_out(o, sg, x, attn_w_out[j].astype(BF16), gate, tm=_tile(t, 256))
        if last and kind != 0:
            x = _final_norm(x, final_row, tm=_tile(t, 256))
    return x
```

```python
import functools
import math

import jax
import jax.numpy as jnp
from jax import lax
from jax.experimental import pallas as pl
from jax.experimental.pallas import tpu as pltpu

N_MIXERS = 3
RWKV_HEAD = 64
ATTN_HEAD = 128
ATTN_GROUP = 2
GRID_W = 64
ROPE_THETA = 10000.0
NORM_EPS = 1e-6
LN_EPS = 1e-5
GN_EPS = 64e-5

V7X_VMEM_BYTES = 64 * 1024 * 1024
VMEM_LIMIT_BYTES = V7X_VMEM_BYTES - 8 * 1024 * 1024
LANES = 128
SUBLANES = 8
INV_BLOCK = 16
SCAN_CHUNK = 64
HALO = 16

F32 = jnp.float32
BF16 = jnp.bfloat16
HI = lax.Precision.HIGHEST


def _cparams(*sem):
    return pltpu.CompilerParams(dimension_semantics=sem, vmem_limit_bytes=VMEM_LIMIT_BYTES)


def _const_spec(shape):
    nd = len(shape)
    return pl.BlockSpec(shape, lambda *_: (0,) * nd, pipeline_mode=pl.Buffered(1))


def _mod_spec(arr):
    nb, _, d = arr.shape
    if nb == 1:
        return pl.BlockSpec((1, 1, d), lambda b, i: (0, 0, 0))
    return pl.BlockSpec((1, 1, d), lambda b, i: (b, 0, 0))


def _sigmoid(x):
    return 1.0 / (1.0 + jnp.exp(-x))


def _silu(x):
    return x * _sigmoid(x)


def _norm_mod(x, g, shift, scale):
    ms = jnp.mean(x * x, axis=-1, keepdims=True)
    n = x * lax.rsqrt(ms + NORM_EPS) * g
    return n * (1.0 + scale) + shift


def _dot(a, b):
    return jnp.dot(a, b, preferred_element_type=F32)


def _dot_hi(a, b):
    return jnp.dot(a, b, preferred_element_type=F32, precision=HI)


def _iota_div(shape, axis, n):
    assert n & (n - 1) == 0
    return lax.shift_right_logical(lax.broadcasted_iota(jnp.int32, shape, axis), n.bit_length() - 1)


def _iota_mod(shape, axis, n):
    assert n & (n - 1) == 0
    return lax.broadcasted_iota(jnp.int32, shape, axis) & (n - 1)


def _head_ones(width, head):
    r = _iota_div((width, width), 0, head)
    c = _iota_div((width, width), 1, head)
    return jnp.where(r == c, 1.0, 0.0).astype(BF16)


def _head_sum(x, ones):
    hi = x.astype(BF16)
    lo = (x - hi.astype(F32)).astype(BF16)
    rows = x.shape[0]
    s = _dot(jnp.concatenate([hi, lo], axis=0), ones)
    return s[:rows] + s[rows:]


def _modulation_body(c_ref, w_ref, b_ref, o_ref):
    s = _silu(c_ref[...])
    o_ref[0] = _dot_hi(s, w_ref[0]) + b_ref[0]


def _modulation(cond, mod_w, mod_b):
    depth, d, d3 = mod_w.shape
    rows = cond.shape[0]
    return pl.pallas_call(
        _modulation_body,
        grid=(depth, d3 // d),
        in_specs=[
            pl.BlockSpec((rows, d), lambda i, j: (0, 0)),
            pl.BlockSpec((1, d, d), lambda i, j: (i, 0, j)),
            pl.BlockSpec((1, 1, d), lambda i, j: (i, 0, j)),
        ],
        out_specs=pl.BlockSpec((1, rows, d), lambda i, j: (i, 0, j)),
        out_shape=jax.ShapeDtypeStruct((depth, rows, d3), F32),
        compiler_params=_cparams("parallel", "parallel"),
        name="modulation",
    )(cond, mod_w, mod_b.reshape(depth, 1, d3))


def _conv_in_body(x_ref, g_ref, sh_ref, sc_ref, w_ref, y_ref, sg_ref, *, tn):
    d = x_ref.shape[-1]
    h = _norm_mod(x_ref[0], g_ref[...], sh_ref[0], sc_ref[0]).astype(BF16)
    for n in range(d // tn):
        cols = slice(n * tn, (n + 1) * tn)
        a = _dot(h, w_ref[0, :, cols])
        b = _dot(h, w_ref[1, :, cols])
        gt = _dot(h, w_ref[2, :, cols])
        y_ref[0, :, cols] = a * _sigmoid(b)
        sg_ref[0, :, cols] = _silu(gt)


def _conv_in(x, g, shift, scale, w3, *, tm):
    bsz, t, d = x.shape
    tn = min(d, 256)
    row = pl.BlockSpec((1, tm, d), lambda b, i: (b, i, 0))
    return pl.pallas_call(
        functools.partial(_conv_in_body, tn=tn),
        grid=(bsz, t // tm),
        in_specs=[row, _const_spec((1, d)), _mod_spec(shift), _mod_spec(scale), _const_spec((3, d, d))],
        out_specs=[row, row],
        out_shape=[jax.ShapeDtypeStruct((bsz, t, d), F32)] * 2,
        compiler_params=_cparams("parallel", "parallel"),
        name="conv_in",
    )(x, g, shift, scale, w3)


def _conv_mid_body(yp_ref, yc_ref, yn_ref, sg_ref, x_ref, dw_ref, db_ref, lg_ref, lb_ref, w_ref, gate_ref,
                   *rest, width, final):
    if final:
        fg_ref, o_ref = rest
    else:
        (o_ref,) = rest
    i = pl.program_id(1)
    nt = pl.num_programs(1)
    tq = yc_ref.shape[1]
    prev = jnp.where(i > 0, yp_ref[0], 0.0)
    nxt = jnp.where(i < nt - 1, yn_ref[0], 0.0)
    win = jnp.concatenate([prev, yc_ref[0], nxt], axis=0)
    off = HALO - width // 2
    acc = db_ref[...]
    for res in range(SUBLANES):
        part = None
        for k in range(width):
            if (off + k) % SUBLANES != res:
                continue
            base = off + k - res
            term = win[base:base + tq + SUBLANES] * dw_ref[k:k + 1, :]
            part = term if part is None else part + term
        if part is not None:
            acc = acc + part[res:res + tq]
    mean = jnp.mean(acc, axis=-1, keepdims=True)
    cen = acc - mean
    var = jnp.mean(cen * cen, axis=-1, keepdims=True)
    yn = cen * lax.rsqrt(var + LN_EPS) * lg_ref[...] + lb_ref[...]
    u = (_silu(yn) * sg_ref[0]).astype(BF16)
    out = x_ref[0] + gate_ref[0] * _dot(u, w_ref[...])
    if final:
        ms = jnp.mean(out * out, axis=-1, keepdims=True)
        out = out * lax.rsqrt(ms + NORM_EPS) * fg_ref[...]
    o_ref[0] = out


def _conv_mid(y, sg, x, dw, db, ln_g, ln_b, w_out, gate, final_g, *, tq):
    bsz, t, d = x.shape
    width = dw.shape[0]
    assert width // 2 <= HALO and t % HALO == 0 and tq % HALO == 0
    r = tq // HALO
    nh = t // HALO
    row = pl.BlockSpec((1, tq, d), lambda b, i: (b, i, 0))
    prev = pl.BlockSpec((1, HALO, d), lambda b, i: (b, jnp.maximum(i * r - 1, 0), 0))
    nxt = pl.BlockSpec((1, HALO, d), lambda b, i: (b, jnp.minimum((i + 1) * r, nh - 1), 0))
    final = final_g is not None
    in_specs = [prev, row, nxt, row, row, _const_spec((width, d)), _const_spec((1, d)), _const_spec((1, d)),
                _const_spec((1, d)), _const_spec((d, d)), _mod_spec(gate)]
    args = [y, y, y, sg, x, dw, db, ln_g, ln_b, w_out, gate]
    if final:
        in_specs.append(_const_spec((1, d)))
        args.append(final_g)
    return pl.pallas_call(
        functools.partial(_conv_mid_body, width=width, final=final),
        grid=(bsz, t // tq),
        in_specs=in_specs,
        out_specs=row,
        out_shape=jax.ShapeDtypeStruct((bsz, t, d), F32),
        compiler_params=_cparams("parallel", "parallel"),
        name="conv_mid",
    )(*args)


def _rwkv_feat_body(xp_ref, xc_ref, xn_ref, g_ref, sh_ref, sc_ref, mu_ref, wr_ref, wk_ref, wv_ref, wg_ref,
                    w1_ref, w2_ref, a1_ref, a2_ref, w0_ref, a0_ref, kk_ref_in, ka_ref,
                    r_ref, v_ref, g_out_ref, kk_ref, lw0_ref, lw1_ref, b0_ref, b1_ref, kd0_ref, kd1_ref):
    i = pl.program_id(1)
    nt = pl.num_programs(1)
    tm, d = xc_ref.shape[1], xc_ref.shape[2]
    g, sh, sc = g_ref[...], sh_ref[0], sc_ref[0]
    h = _norm_mod(xc_ref[0], g, sh, sc)
    hp = _norm_mod(xp_ref[0], g, sh, sc)[7:8]
    hn = _norm_mod(xn_ref[0], g, sh, sc)[0:1]
    hp = jnp.where(i > 0, hp, 0.0)
    hn = jnp.where(i < nt - 1, hn, 0.0)
    rows = lax.broadcasted_iota(jnp.int32, (tm, d), 0)
    hm1 = jnp.where(rows == 0, hp, pltpu.roll(h, 1, 0))
    hp1 = jnp.where(rows == tm - 1, hn, pltpu.roll(h, tm - 1, 0))
    xx = 0.5 * (hm1 + hp1) - h

    def lerp(n):
        return (h + xx * mu_ref[n:n + 1, :]).astype(BF16)

    r = _dot(lerp(0), wr_ref[...])
    k = _dot(lerp(2), wk_ref[...])
    v = _dot(lerp(3), wv_ref[...])
    r_ref[0] = r.astype(BF16)
    v_ref[0] = v.astype(BF16)
    g_out_ref[0] = _dot(lerp(5), wg_ref[...]).astype(BF16)
    kx = k * kk_ref_in[...]
    ones = _head_ones(LANES, RWKV_HEAD)
    parts = []
    for c in range(d // LANES):
        sl = slice(c * LANES, (c + 1) * LANES)
        q = kx[:, sl]
        parts.append(q * lax.rsqrt(_head_sum(q * q, ones) + 1e-12))
    kk = parts[0] if len(parts) == 1 else jnp.concatenate(parts, axis=1)
    kk_ref[0] = kk.astype(BF16)
    tw =jnp.tanh(_dot(lerp(1), w1_ref[...])).astype(BF16)
    ta = _dot(lerp(4), a1_ref[...]).astype(BF16)
    for dr, (lw_ref, b_ref, kd_ref) in enumerate(((lw0_ref, b0_ref, kd0_ref), (lw1_ref, b1_ref, kd1_ref))):
        z = w0_ref[dr:dr + 1, :] + _dot(tw, w2_ref[dr])
        sp = jnp.maximum(-z, 0.0) + jnp.log(1.0 + jnp.exp(-jnp.abs(z)))
        lw_ref[0] = -jnp.exp(-sp - 0.5)
        a = _sigmoid(a0_ref[dr:dr + 1, :] + _dot(ta, a2_ref[dr]))
        b_ref[0] = (kk * a).astype(BF16)
        kd_ref[0] = (k * (1.0 + (a - 1.0) * ka_ref[...])).astype(BF16)


def _rwkv_feat(x, g, shift, scale, p, *, tm):
    bsz, t, d = x.shape
    assert t % 8 == 0 and tm % 8 == 0
    r8 = tm // 8
    n8 = t // 8
    row = pl.BlockSpec((1, tm, d), lambda b, i: (b, i, 0))
    prev = pl.BlockSpec((1, 8, d), lambda b, i: (b, jnp.maximum(i * r8 - 1, 0), 0))
    nxt = pl.BlockSpec((1, 8, d), lambda b, i: (b, jnp.minimum((i + 1) * r8, n8 - 1), 0))
    lora = p["w1"].shape[-1]
    in_specs = [prev, row, nxt, _const_spec((1, d)), _mod_spec(shift), _mod_spec(scale), _const_spec((6, d)),
                _const_spec((d, d)), _const_spec((d, d)), _const_spec((d, d)), _const_spec((d, d)),
                _const_spec((d, lora)), _const_spec((2, lora, d)), _const_spec((d, lora)), _const_spec((2, lora, d)),
                _const_spec((2, d)), _const_spec((2, d)), _const_spec((1, d)), _const_spec((1, d))]
    return pl.pallas_call(
        _rwkv_feat_body,
        grid=(bsz, t // tm),
        in_specs=in_specs,
        out_specs=[row] * 10,
        out_shape=[jax.ShapeDtypeStruct((bsz, t, d), dt) for dt in [BF16] * 4 + [F32] * 2 + [BF16] * 4],
        compiler_params=_cparams("parallel", "parallel"),
        name="rwkv_feat",
    )(x, x, x, g, shift, scale, p["mu"], p["w_r"], p["w_k"], p["w_v"], p["w_g"],
      p["w1"], p["w2"], p["a1"], p["a2"], p["w0"], p["a0"], p["k_k"], p["k_a"])


def _scan_body(lw_ref, kk_ref, b_ref, kd_ref, v_ref, r_ref, s0_ref, o_ref, s1_ref, h_ref, *, reverse, gl):
    c = pl.program_id(1)
    nc = pl.num_programs(1)
    nb, cs_len, d = lw_ref.shape
    n = RWKV_HEAD
    assert cs_len == n, "block-diagonal packing assumes chunk == head size"
    hp = gl // n

    @pl.when(c == 0)
    def _():
        h_ref[...] = s0_ref[...]

    ri = lax.broadcasted_iota(jnp.int32, (cs_len, cs_len), 0)
    ci = lax.broadcasted_iota(jnp.int32, (cs_len, cs_len), 1)
    tri = jnp.where((ci >= ri) if reverse else (ci <= ri), 1.0, 0.0).astype(BF16)
    kkt, kh, bh, rt, kb, bb, vv, g_tot = ([] for _ in range(8))
    for bi in range(nb):
        lw = lw_ref[bi]
        lw_hi = lw.astype(BF16)
        rem = lw - lw_hi.astype(F32)
        lw_mid = rem.astype(BF16)
        lw_lo = (rem - lw_mid.astype(F32)).astype(BF16)
        cum = _dot(tri, lw_hi) + _dot(tri, lw_mid) + _dot(tri, lw_lo)
        tot = cum[0:1] if reverse else cum[cs_len - 1:cs_len]
        g_inv = jnp.exp(-cum)
        g_tot.append(jnp.exp(tot))
        kkt.append(kk_ref[bi].astype(F32) * jnp.exp(cum - lw))
        kh.append(kd_ref[bi].astype(F32) * g_inv)
        bh.append(b_ref[bi].astype(F32) * g_inv)
        rt.append(r_ref[bi].astype(F32) * jnp.exp(cum))
        kb.append(kh[bi] * g_tot[bi])
        bb.append(bh[bi] * g_tot[bi])
        vv.append(v_ref[bi].astype(F32))

    bdmask = _iota_div((gl, gl), 0, n) == _iota_div((gl, gl), 1, n)
    i_s = lax.broadcasted_iota(jnp.int32, (cs_len, gl), 0)
    j_s = _iota_mod((cs_len, gl), 1, n)
    strict = (j_s > i_s) if reverse else (j_s < i_s)
    incl = (j_s >= i_s) if reverse else (j_s <= i_s)
    eye = j_s == i_s
    shift = INV_BLOCK.bit_length() - 1
    dblock = lax.shift_right_logical(i_s, shift) == lax.shift_right_logical(j_s, shift)
    lane_head = _iota_div((n, gl), 1, n)

    def bd(y):
        return jnp.where(bdmask, jnp.concatenate([y] * hp, axis=0), 0.0).astype(BF16)

    def mm(a, bm):
        return _dot(a.astype(BF16), bm)

    def mm_t(a, bm):
        return lax.dot_general(a.astype(BF16), bm, (((1,), (1,)), ((), ())), preferred_element_type=F32)

    def collapse(m):
        out = jnp.where(lane_head == 0, m[0:n], 0.0)
        for hh in range(1, hp):
            out = out + jnp.where(lane_head == hh, m[hh * n:(hh + 1) * n], 0.0)
        return out

    groups = [(bi, slice(g * gl, (g + 1) * gl)) for bi in range(nb) for g in range(d // gl)]
    cat0 = functools.partial(jnp.concatenate, axis=0)
    lhs = [cat0([kkt[bi][:, sl], rt[bi][:, sl]]) for bi, sl in groups]
    ab = [mm_t(l, bd(bh[bi][:, sl])) for l, (bi, sl) in zip(lhs, groups)]
    ak = [mm_t(l, bd(kh[bi][:, sl])) for l, (bi, sl) in zip(lhs, groups)]
    a_m = [jnp.where(strict, y[:cs_len], 0.0) for y in ab]
    m2 = [jnp.where(incl, y[cs_len:], 0.0) for y in ab]
    b_m = [jnp.where(strict, y[:cs_len], 0.0) for y in ak]
    m1 = [jnp.where(incl, y[cs_len:], 0.0) for y in ak]
    ident = jnp.where(eye, 1.0, 0.0)
    a_d = [jnp.where(dblock, y, 0.0) for y in a_m]
    xp = [-y for y in a_d]
    tm_ = [ident + y for y in xp]
    xp = [mm(y, bd(y)) for y in xp]
    for _ in range(int(math.log2(INV_BLOCK)) - 2):
        st = [mm(cat0([y, z]), bd(y)) for y, z in zip(xp, tm_)]
        xp = [y[:cs_len] for y in st]
        tm_ = [z + y[cs_len:] for y, z in zip(st, tm_)]
    t_d = [z + mm(z, bd(y)) for y, z in zip(xp, tm_)]
    nn = [mm(z, bd(y - w)) for z, y, w in zip(t_d, a_m, a_d)]
    vm = [ident - y for y in nn]
    for _ in range(int(math.log2(cs_len // INV_BLOCK)) - 1):
        nn = [mm(y, bd(y)) for y in nn]
        vm = [z + mm(z, bd(y)) for y, z in zip(nn, vm)]
    tm_ = [mm(z, bd(y)) for y, z in zip(t_d, vm)]
    wt = [mm(z, bd(kkt[bi][:, sl])) for z, (bi, sl) in zip(tm_, groups)]
    sv = [mm(cat0([y, z]), bd(vv[bi][:, sl])) for y, z, (bi, sl) in zip(b_m, m1, groups)]
    u0 = [mm(z, bd(y[:cs_len])) for y, z in zip(sv, tm_)]
    rbar = [rt[bi][:, sl] - mm(y, bd(z)) for y, z, (bi, sl) in zip(m2, wt, groups)]
    o0 = [y[cs_len:] - mm(z, bd(w)) for y, z, w in zip(sv, m2, u0)]
    bbt = [bb[bi][:, sl].T for bi, sl in groups]
    kbt = [kb[bi][:, sl].T for bi, sl in groups]
    pc = [mm(y, z.astype(BF16)) for y, z in zip(bbt, wt)]
    gc = [mm(jnp.concatenate([y, -z], axis=1), cat0([vv[bi][:, sl], w]).astype(BF16))
          for y, z, w, (bi, sl) in zip(kbt, bbt, u0, groups)]
    p_s = [jnp.where(eye, g_tot[bi][:, sl], 0.0) - collapse(y) for y, (bi, sl) in zip(pc, groups)]
    g_s = [collapse(y) for y in gc]
    so = [mm(cat0([y, z]), bd(h_ref[bi, :, sl])) for y, z, (bi, sl) in zip(rbar, p_s, groups)]
    for y, z, w, (bi, sl) in zip(so, o0, g_s, groups):
        o_ref[bi, :, sl] = y[:cs_len] + z
        h_ref[bi, :, sl] = y[cs_len:] + w

    @pl.when(c == nc - 1)
    def _():
        s1_ref[...] = h_ref[...]


def _rwkv_scan(lw, kk, b, kd, v, r, state, *, reverse):
    bsz, t, d = lw.shape
    cs = SCAN_CHUNK
    nc = t // cs
    gl = min(d, 256)
    nb = next(n for n in (4, 2, 1) if bsz % n == 0)
    if reverse:
        row = pl.BlockSpec((nb, cs, d), lambda bi, c: (bi, nc - 1 - c, 0))
    else:
        row = pl.BlockSpec((nb, cs, d), lambda bi, c: (bi, c, 0))
    st = pl.BlockSpec((nb, RWKV_HEAD, d), lambda bi, c: (bi, 0, 0))
    return pl.pallas_call(
        functools.partial(_scan_body, reverse=reverse, gl=gl),
        grid=(bsz // nb, nc),
        in_specs=[row] * 6 + [st],
        out_specs=[row, st],
        out_shape=[jax.ShapeDtypeStruct((bsz, t, d), F32), jax.ShapeDtypeStruct((bsz, RWKV_HEAD, d), F32)],
        scratch_shapes=[pltpu.VMEM((nb, RWKV_HEAD, d), F32)],
        compiler_params=_cparams("parallel", "arbitrary"),
        name="rwkv_scan_rev" if reverse else "rwkv_scan_fwd",
    )(lw, kk, b, kd, v, r, state)


def _rwkv_out_body(o0_ref, o1_ref, r_ref, kd0_ref, kd1_ref, v_ref, g_ref, x_ref, rk_ref, lg_ref, lb_ref, w_ref,
                   gate_ref, out_ref, y_ref):
    d = x_ref.shape[-1]
    ones = _head_ones(LANES, RWKV_HEAD)
    inv_n = 1.0 / RWKV_HEAD
    for c in range(d // LANES):
        sl = slice(c * LANES, (c + 1) * LANES)
        o = o0_ref[0, :, sl] + o1_ref[0, :, sl]
        mean = _head_sum(o, ones) * inv_n
        cen = o - mean
        var = _head_sum(cen * cen, ones) * inv_n
        on = cen * lax.rsqrt(var + GN_EPS) * lg_ref[:, sl] + lb_ref[:, sl]
        ksum = kd0_ref[0, :, sl].astype(F32) + kd1_ref[0, :, sl].astype(F32)
        bonus = _head_sum(r_ref[0, :, sl].astype(F32) * ksum * rk_ref[:, sl], ones) * v_ref[0, :, sl].astype(F32)
        y_ref[:, sl] = ((on + bonus) * _silu(g_ref[0, :, sl].astype(F32))).astype(BF16)
    out_ref[0] = x_ref[0] + gate_ref[0] * _dot(y_ref[...], w_ref[...])


def _rwkv_out(o0, o1, r, kd0, kd1, v, g, x, r_k, ln_g, ln_b, w_o, gate, *, tm):
    bsz, t, d = x.shape
    row = pl.BlockSpec((1, tm, d), lambda b, i: (b, i, 0))
    return pl.pallas_call(
        _rwkv_out_body,
        grid=(bsz, t // tm),
        in_specs=[row] * 8 + [_const_spec((1, d))] * 3 + [_const_spec((d, d)), _mod_spec(gate)],
        out_specs=row,
        out_shape=jax.ShapeDtypeStruct((bsz, t, d), F32),
        scratch_shapes=[pltpu.VMEM((tm, d), BF16)],
        compiler_params=_cparams("parallel", "parallel"),
        name="rwkv_out",
    )(o0, o1, r, kd0, kd1, v, g, x, r_k, ln_g, ln_b, w_o, gate)


def _head_norm(x, g):
    ms = jnp.mean(x * x, axis=-1, keepdims=True)
    return x * lax.rsqrt(ms + NORM_EPS) * g


def _rope(x, cos, sin_signed):
    lane = _iota_mod(x.shape, 1, ATTN_HEAD // 2)
    swapped = jnp.where(lane < ATTN_HEAD // 4, pltpu.roll(x, ATTN_HEAD - ATTN_HEAD // 4, 1),
                        pltpu.roll(x, ATTN_HEAD // 4, 1))
    return x * cos + swapped * sin_signed


def _attn_in_body(x_ref, g_ref, sh_ref, sc_ref, w_ref, qg_ref, kg_ref, cos_ref, sin_ref, *outs,
                  qw, kvw, with_q, rope):
    h = _norm_mod(x_ref[0], g_ref[...], sh_ref[0], sc_ref[0]).astype(BF16)
    hd = ATTN_HEAD
    if with_q:
        q_ref, k_ref, v_ref, gt_ref = outs
        base_k = qw
    else:
        k_ref, v_ref = outs
        base_k = 0
    cos, sin = cos_ref[...], sin_ref[...]
    tn = 2 * hd
    if with_q:
        for n in range(qw // tn):
            qq = _dot(h, w_ref[:, n * tn:(n + 1) * tn])
            for half in range(2):
                q = _head_norm(qq[:, half * hd:(half + 1) * hd], qg_ref[...])
                q_ref[0, :, n * tn + half * hd:n * tn + (half + 1) * hd] = (
                    _rope(q, cos, sin) * (hd ** -0.5)).astype(BF16)
    for n in range(kvw // tn):
        kk2 = _dot(h, w_ref[:, base_k + n * tn:base_k + (n + 1) * tn])
        for half in range(2):
            k = _head_norm(kk2[:, half * hd:(half + 1) * hd], kg_ref[...])
            if rope:
                k = _rope(k, cos, sin)
            k_ref[0, :, n * tn + half * hd:n * tn + (half + 1) * hd] = k.astype(BF16)
    for n in range(kvw // tn):
        v_ref[0, :, n * tn:(n + 1) * tn] = _dot(
            h, w_ref[:, base_k + kvw + n * tn:base_k + kvw + (n + 1) * tn]).astype(BF16)
    if with_q:
        for n in range(qw // tn):
            gt = _dot(h, w_ref[:, qw + 2 * kvw + n * tn:qw + 2 * kvw + (n + 1) * tn])
            gt_ref[0, :, n * tn:(n + 1) * tn] = _silu(gt).astype(BF16)


def _attn_in(x, g, shift, scale, w, q_g, k_g, cos, sin, *, qw, kvw, with_q, rope, tm):
    bsz, t, d = x.shape
    row = pl.BlockSpec((1, tm, d), lambda b, i: (b, i, 0))
    tab = pl.BlockSpec((tm, ATTN_HEAD), lambda b, i: (i, 0))
    nw = w.shape[1]
    outs = []
    shapes = []
    if with_q:
        outs.append(pl.BlockSpec((1, tm, qw), lambda b, i: (b, i, 0)))
        shapes.append(jax.ShapeDtypeStruct((bsz, t, qw), BF16))
    for _ in range(2):
        outs.append(pl.BlockSpec((1, tm, kvw), lambda b, i: (b, i, 0)))
        shapes.append(jax.ShapeDtypeStruct((bsz, t, kvw), BF16))
    if with_q:
        outs.append(pl.BlockSpec((1, tm, qw), lambda b, i: (b, i, 0)))
        shapes.append(jax.ShapeDtypeStruct((bsz, t, qw), BF16))
    return pl.pallas_call(
        functools.partial(_attn_in_body, qw=qw, kvw=kvw, with_q=with_q, rope=rope),
        grid=(bsz, t // tm),
        in_specs=[row, _const_spec((1, d)), _mod_spec(shift), _mod_spec(scale), _const_spec((d, nw)),
                  _const_spec((1, ATTN_HEAD)), _const_spec((1, ATTN_HEAD)), tab, tab],
        out_specs=outs,
        out_shape=shapes,
        compiler_params=_cparams("parallel", "parallel"),
        name="attn_in" if with_q else "attn_in_ctx",
    )(x, g, shift, scale, w, q_g, k_g, cos, sin)


def _flash_body(q_ref, k_ref, vt_ref, o_ref, *, sub):
    k, vt = k_ref[0], vt_ref[0]
    tq = q_ref.shape[1]
    units = [(slice(r, r + sub), slice(gq * ATTN_HEAD, (gq + 1) * ATTN_HEAD))
             for r in range(0, tq, sub) for gq in range(ATTN_GROUP)]
    nt = (((1,), (1,)), ((), ()))
    st = [lax.dot_general(k, q_ref[0, rs, sl], nt, preferred_element_type=F32) for rs, sl in units]
    m = [jnp.max(s, axis=0, keepdims=True) for s in st]
    p = [jnp.exp(s - mx) for s, mx in zip(st, m)]
    den = [jnp.sum(y, axis=0, keepdims=True) for y in p]
    ot = [_dot(vt, y.astype(BF16)) for y in p]
    for (rs, sl), y, dn in zip(units, ot, den):
        o_ref[0, rs, sl] = (y / dn).T.astype(BF16)


def _flash(q, k_all, vt_all, *, tq):
    bsz, t, qw = q.shape
    keys = k_all.shape[1]
    hkv = k_all.shape[2] // ATTN_HEAD
    gw = ATTN_GROUP * ATTN_HEAD
    qspec = pl.BlockSpec((1, tq, gw), lambda b, hh, i: (b, i, hh))
    kspec = pl.BlockSpec((1, keys, ATTN_HEAD), lambda b, hh, i: (b, 0, hh))
    vspec = pl.BlockSpec((1, ATTN_HEAD, keys), lambda b, hh, i: (b, hh, 0))
    return pl.pallas_call(
        functools.partial(_flash_body, sub=min(tq, 256)),
        grid=(bsz, hkv, t // tq),
        in_specs=[qspec, kspec, vspec],
        out_specs=qspec,
        out_shape=jax.ShapeDtypeStruct((bsz, t, qw), BF16),
        compiler_params=_cparams("parallel", "parallel", "arbitrary"),
        name="flash_gqa",
    )(q, k_all, vt_all)


def _gated_out_body(o_ref, sg_ref, x_ref, w_ref, gate_ref, out_ref):
    u = (o_ref[0].astype(F32) * sg_ref[0].astype(F32)).astype(BF16)
    out_ref[0] = x_ref[0] + gate_ref[0] * _dot(u, w_ref[...])


def _gated_out(o, sg, x, w, gate, *, tm):
    bsz, t, d = x.shape
    kdim = o.shape[-1]
    row = pl.BlockSpec((1, tm, d), lambda b, i: (b, i, 0))
    wide = pl.BlockSpec((1, tm, kdim), lambda b, i: (b, i, 0))
    return pl.pallas_call(
        _gated_out_body,
        grid=(bsz, t // tm),
        in_specs=[wide, wide, row, _const_spec((kdim, d)), _mod_spec(gate)],
        out_specs=row,
        out_shape=jax.ShapeDtypeStruct((bsz, t, d), F32),
        compiler_params=_cparams("parallel", "parallel"),
        name="attn_out",
    )(o, sg, x, w, gate)


def _final_norm_body(x_ref, g_ref, o_ref):
    x = x_ref[0]
    ms = jnp.mean(x * x, axis=-1, keepdims=True)
    o_ref[0] = x * lax.rsqrt(ms + NORM_EPS) * g_ref[...]


def _final_norm(x, g, *, tm):
    bsz, t, d = x.shape
    row = pl.BlockSpec((1, tm, d), lambda b, i: (b, i, 0))
    return pl.pallas_call(
        _final_norm_body,
        grid=(bsz, t // tm),
        in_specs=[row, _const_spec((1, d))],
        out_specs=row,
        out_shape=jax.ShapeDtypeStruct((bsz, t, d), F32),
        compiler_params=_cparams("parallel", "parallel"),
        name="final_norm",
    )(x, g)


def _rope_tables(t):
    axis_dim = ATTN_HEAD // 2
    rows = t // GRID_W
    row = jnp.repeat(jnp.arange(rows), GRID_W).astype(F32)
    col = (jnp.arange(rows * GRID_W) % GRID_W).astype(F32)
    inv = 1.0 / (ROPE_THETA ** (jnp.arange(0, axis_dim, 2, dtype=F32) / axis_dim))
    ar, ac = row[:, None] * inv, col[:, None] * inv
    cos = jnp.concatenate([jnp.cos(ar), jnp.cos(ar), jnp.cos(ac), jnp.cos(ac)], axis=-1)
    sin = jnp.concatenate([-jnp.sin(ar), jnp.sin(ar), -jnp.sin(ac), jnp.sin(ac)], axis=-1)
    return cos, sin


def _tile(t, want):
    tm = min(t, want)
    assert t % tm == 0
    return tm


def kernel(x, c, ctx, c_ctx, norm_g, mod_w, mod_b, conv_w_in, conv_dw, conv_db, conv_ln_g, conv_ln_b, conv_w_out, rwkv_mu, rwkv_w_r, rwkv_w_k, rwkv_w_v, rwkv_w_g, rwkv_w0, rwkv_w1, rwkv_w2, rwkv_a0, rwkv_a1, rwkv_a2, rwkv_k_k, rwkv_k_a, rwkv_r_k, rwkv_ln_g, rwkv_ln_b, rwkv_w_o, attn_w_in, attn_q_g, attn_k_g, attn_w_out, final_g):
    bsz, t, d = x.shape
    lc = ctx.shape[1]
    depth = mod_w.shape[0]
    qw = (d // 64) * ATTN_HEAD
    kvw = qw // ATTN_GROUP

    pad = (-(bsz + 1)) % 8
    cond = jnp.concatenate([c, c_ctx[None, :], jnp.zeros((pad, d), F32)], axis=0)
    mods = _modulation(cond, mod_w, mod_b)
    cos, sin = _rope_tables(t)
    final_row = final_g.reshape(1, d)

    xc = ctx
    for i in range(depth):
        kind, j = i % N_MIXERS, i // N_MIXERS
        ctx_out = any(l % N_MIXERS != 0 for l in range(i + 1, depth))
        ctx_in = ctx_out or kind != 0
        last = i == depth - 1
        m = mods[i]
        shift, scale, gate = (m[:bsz, None, k * d:(k + 1) * d] for k in range(3))
        shift_c, scale_c, gate_c = (m[bsz:bsz + 1, None, k * d:(k + 1) * d] for k in range(3))
        g = norm_g[i].reshape(1, d)
        if kind == 0:
            w3 = conv_w_in[j].reshape(d, 3, d).transpose(1, 0, 2).astype(BF16)
            w_out = conv_w_out[j].astype(BF16)
            cp = (conv_dw[j], conv_db[j].reshape(1, d), conv_ln_g[j].reshape(1, d), conv_ln_b[j].reshape(1, d), w_out)
            y, sg = _conv_in(x, g, shift, scale, w3, tm=_tile(t, 512))
            x_new = _conv_mid(y, sg, x, *cp, gate, final_row if last else None, tq=_tile(t, 256))
            if ctx_out:
                yc, sgc = _conv_in(xc, g, shift_c, scale_c, w3, tm=_tile(lc, 256))
                xc = _conv_mid(yc, sgc, xc, *cp, gate_c, None, tq=_tile(lc, 256))
            x = x_new
        elif kind == 1:
            lora = rwkv_w1.shape[-1]
            zero = jnp.zeros((lora, d), F32)

            def padded(w2):
                return jnp.stack([jnp.concatenate([w2[0], zero], 0), jnp.concatenate([zero, w2[1]], 0)]).astype(BF16)

            p = dict(
                mu=rwkv_mu[j], w_r=rwkv_w_r[j].astype(BF16), w_k=rwkv_w_k[j].astype(BF16),
                w_v=rwkv_w_v[j].astype(BF16), w_g=rwkv_w_g[j].astype(BF16),
                w1=jnp.concatenate([rwkv_w1[j, 0], rwkv_w1[j, 1]], axis=-1).astype(BF16), w2=padded(rwkv_w2[j]),
                a1=jnp.concatenate([rwkv_a1[j, 0], rwkv_a1[j, 1]], axis=-1).astype(BF16), a2=padded(rwkv_a2[j]),
                w0=rwkv_w0[j], a0=rwkv_a0[j], k_k=rwkv_k_k[j].reshape(1, d), k_a=rwkv_k_a[j].reshape(1, d))
            p["w1"] = p["w1"].reshape(d, 2 * lora)
            fl = _rwkv_feat(x, g, shift, scale, p, tm=_tile(t, 256))
            fc = _rwkv_feat(xc, g, shift_c, scale_c, p, tm=_tile(lc, 256))
            state0 = jnp.zeros((bsz, RWKV_HEAD, d), F32)
            o_l, o_c = [], []
            for dr, rev in ((0, False), (1, True)):
                oc_d, st = _rwkv_scan(fc[4 + dr], fc[3], fc[6 + dr], fc[8 + dr], fc[1], fc[0], state0, reverse=rev)
                ol_d, _ = _rwkv_scan(fl[4 + dr], fl[3], fl[6 + dr], fl[8 + dr], fl[1], fl[0], st, reverse=rev)
                o_l.append(ol_d)
                o_c.append(oc_d)
            ro = (rwkv_r_k[j].reshape(1, d), rwkv_ln_g[j].reshape(1, d), rwkv_ln_b[j].reshape(1, d),
                  rwkv_w_o[j].astype(BF16))
            x_new = _rwkv_out(o_l[0], o_l[1], fl[0], fl[8], fl[9], fl[1], fl[2], x, *ro, gate, tm=_tile(t, 256))
            if ctx_out:
                xc = _rwkv_out(o_c[0], o_c[1], fc[0], fc[8], fc[9], fc[1], fc[2], xc, *ro, gate_c, tm=_tile(lc, 256))
            x = x_new
        else:
            w_in = attn_w_in[j].astype(BF16)
            qg, kg = attn_q_g[j].reshape(1, ATTN_HEAD), attn_k_g[j].reshape(1, ATTN_HEAD)
            q, k, v, sg = _attn_in(x, g, shift, scale, w_in, qg, kg, cos, sin, qw=qw, kvw=kvw, with_q=True,
                                   rope=True, tm=_tile(t, 256))
            if ctx_out:
                raise NotImplementedError("context-stream attention output is not needed at this depth")
            no_rope = jnp.zeros((lc, ATTN_HEAD), F32)
            kc, vc = _attn_in(xc, g, shift_c, scale_c, w_in[:, qw:qw + 2 * kvw], qg, kg, no_rope, no_rope,
                              qw=qw, kvw=kvw, with_q=False, rope=False, tm=_tile(lc, 256))
            k_all = jnp.concatenate([kc, k], axis=1)
            vt_all = jnp.concatenate([vc, v], axis=1).transpose(0, 2, 1)
            o = _flash(q, k_all, vt_all, tq=_tile(t, 1024))
            x = _gated_out(o, sg, x, attn_w_out[j].astype(BF16), gate, tm=_tile(t, 512))
        if last and kind != 0:
            x = _final_norm(x, final_row, tm=_tile(t, 256))
    return x
```

```python
import functools
import math

import jax
import jax.numpy as jnp
from jax import lax
from jax.experimental import pallas as pl
from jax.experimental.pallas import tpu as pltpu

N_MIXERS = 3
RWKV_HEAD = 64
ATTN_HEAD = 128
ATTN_GROUP = 2
GRID_W = 64
ROPE_THETA = 10000.0
NORM_EPS = 1e-6
LN_EPS = 1e-5
GN_EPS = 64e-5

V7X_VMEM_BYTES = 64 * 1024 * 1024
VMEM_LIMIT_BYTES = V7X_VMEM_BYTES - 8 * 1024 * 1024
LANES = 128
SUBLANES = 8
INV_BLOCK = 16
SCAN_CHUNK = 64
HALO = 16

F32 = jnp.float32
BF16 = jnp.bfloat16
HI = lax.Precision.HIGHEST


def _cparams(*sem):
    return pltpu.CompilerParams(dimension_semantics=sem, vmem_limit_bytes=VMEM_LIMIT_BYTES)


def _const_spec(shape):
    nd = len(shape)
    return pl.BlockSpec(shape, lambda *_: (0,) * nd, pipeline_mode=pl.Buffered(1))


def _mod_spec(arr):
    nb, _, d = arr.shape
    if nb == 1:
        return pl.BlockSpec((1, 1, d), lambda b, i: (0, 0, 0))
    return pl.BlockSpec((1, 1, d), lambda b, i: (b, 0, 0))


def _sigmoid(x):
    return 1.0 / (1.0 + jnp.exp(-x))


def _silu(x):
    return x * _sigmoid(x)


def _norm_mod(x, g, shift, scale):
    ms = jnp.mean(x * x, axis=-1, keepdims=True)
    n = x * lax.rsqrt(ms + NORM_EPS) * g
    return n * (1.0 + scale) + shift


def _dot(a, b):
    return jnp.dot(a, b, preferred_element_type=F32)


def _dot_hi(a, b):
    return jnp.dot(a, b, preferred_element_type=F32, precision=HI)


def _iota_div(shape, axis, n):
    assert n & (n - 1) == 0
    return lax.shift_right_logical(lax.broadcasted_iota(jnp.int32, shape, axis), n.bit_length() - 1)


def _iota_mod(shape, axis, n):
    assert n & (n - 1) == 0
    return lax.broadcasted_iota(jnp.int32, shape, axis) & (n - 1)


def _head_ones(width, head):
    r = _iota_div((width, width), 0, head)
    c = _iota_div((width, width), 1, head)
    return jnp.where(r == c, 1.0, 0.0).astype(BF16)


def _head_sum(x, ones):
    hi = x.astype(BF16)
    lo = (x - hi.astype(F32)).astype(BF16)
    rows = x.shape[0]
    s = _dot(jnp.concatenate([hi, lo], axis=0), ones)
    return s[:rows] + s[rows:]


def _modulation_body(c_ref, w_ref, b_ref, o_ref):
    s = _silu(c_ref[...])
    o_ref[0] = _dot_hi(s, w_ref[0]) + b_ref[0]


def _modulation(cond, mod_w, mod_b):
    depth, d, d3 = mod_w.shape
    rows = cond.shape[0]
    return pl.pallas_call(
        _modulation_body,
        grid=(depth, d3 // d),
        in_specs=[
            pl.BlockSpec((rows, d), lambda i, j: (0, 0)),
            pl.BlockSpec((1, d, d), lambda i, j: (i, 0, j)),
            pl.BlockSpec((1, 1, d), lambda i, j: (i, 0, j)),
        ],
        out_specs=pl.BlockSpec((1, rows, d), lambda i, j: (i, 0, j)),
        out_shape=jax.ShapeDtypeStruct((depth, rows, d3), F32),
        compiler_params=_cparams("parallel", "parallel"),
        name="modulation",
    )(cond, mod_w, mod_b.reshape(depth, 1, d3))


def _conv_layer_body(xp_ref, xc_ref, xn_ref, g_ref, sh_ref, sc_ref, w3_ref, dw_ref, db_ref, lg_ref, lb_ref,
                     w_ref, gate_ref, *rest, width, final, tn):
    if final:
        fg_ref, o_ref, acc_ref, sg_ref = rest
    else:
        o_ref, acc_ref, sg_ref = rest
    i = pl.program_id(1)
    nt = pl.num_programs(1)
    tq, d = xc_ref.shape[1], xc_ref.shape[2]
    g, sh, sc = g_ref[...], sh_ref[0], sc_ref[0]
    xw = jnp.concatenate([xp_ref[0], xc_ref[0], xn_ref[0]], axis=0)
    h = _norm_mod(xw, g, sh, sc).astype(BF16)
    hc = h[HALO:HALO + tq]
    rows = lax.broadcasted_iota(jnp.int32, (tq + 2 * HALO, tn), 0)
    inside = ((rows >= HALO) | (i > 0)) & ((rows < HALO + tq) | (i < nt - 1))
    off = HALO - width // 2

    def project(n):
        cols = slice(n * tn, (n + 1) * tn)
        a = _dot(h, w3_ref[0, :, cols])
        b = _dot(h, w3_ref[1, :, cols])
        gt = _dot(hc, w3_ref[2, :, cols])
        return jnp.where(inside, a * _sigmoid(b), 0.0), gt

    def conv(n, y, gt):
        cols = slice(n * tn, (n + 1) * tn)
        acc = db_ref[:, cols]
        for res in range(SUBLANES):
            part = None
            for k in range(width):
                if (off + k) % SUBLANES != res:
                    continue
                base = off + k - res
                term = y[base:base + tq + SUBLANES] * dw_ref[k:k + 1, cols]
                part = term if part is None else part + term
            if part is not None:
                acc = acc + part[res:res + tq]
        acc_ref[:, cols] = acc
        sg_ref[:, cols] = _silu(gt)

    nblk = d // tn
    nxt = project(0)
    for n in range(nblk):
        cur = nxt
        if n + 1 < nblk:
            nxt = project(n + 1)
        conv(n, *cur)
    acc = acc_ref[...]
    mean = jnp.mean(acc, axis=-1, keepdims=True)
    cen = acc - mean
    var = jnp.mean(cen * cen, axis=-1, keepdims=True)
    yn = cen * lax.rsqrt(var + LN_EPS) * lg_ref[...] + lb_ref[...]
    u = (_silu(yn) * sg_ref[...]).astype(BF16)
    out = xc_ref[0] + gate_ref[0] * _dot(u, w_ref[...])
    if final:
        ms = jnp.mean(out * out, axis=-1, keepdims=True)
        out = out * lax.rsqrt(ms + NORM_EPS) * fg_ref[...]
    o_ref[0] = out


def _conv_layer(x, g, shift, scale, w3, dw, db, ln_g, ln_b, w_out, gate, final_g, *, tq):
    bsz, t, d = x.shape
    width = dw.shape[0]
    assert width // 2 <= HALO and t % HALO == 0 and tq % HALO == 0
    r = tq // HALO
    nh = t // HALO
    tn = min(d, 256)
    row = pl.BlockSpec((1, tq, d), lambda b, i: (b, i, 0))
    prev = pl.BlockSpec((1, HALO, d), lambda b, i: (b, jnp.maximum(i * r - 1, 0), 0))
    nxt = pl.BlockSpec((1, HALO, d), lambda b, i: (b, jnp.minimum((i + 1) * r, nh - 1), 0))
    final = final_g is not None
    in_specs = [prev, row, nxt, _const_spec((1, d)), _mod_spec(shift), _mod_spec(scale), _const_spec((3, d, d)),
                _const_spec((width, d)), _const_spec((1, d)), _const_spec((1, d)), _const_spec((1, d)),
                _const_spec((d, d)), _mod_spec(gate)]
    args = [x, x, x, g, shift, scale, w3, dw, db, ln_g, ln_b, w_out, gate]
    if final:
        in_specs.append(_const_spec((1, d)))
        args.append(final_g)
    return pl.pallas_call(
        functools.partial(_conv_layer_body, width=width, final=final, tn=tn),
        grid=(bsz, t // tq),
        in_specs=in_specs,
        out_specs=row,
        out_shape=jax.ShapeDtypeStruct((bsz, t, d), F32),
        scratch_shapes=[pltpu.VMEM((tq, d), F32), pltpu.VMEM((tq, d), F32)],
        compiler_params=_cparams("parallel", "parallel"),
        name="conv_layer",
    )(*args)


def _rwkv_feat_body(xp_ref, xc_ref, xn_ref, g_ref, sh_ref, sc_ref, mu_ref, wr_ref, wk_ref, wv_ref, wg_ref,
                    w1_ref, w2_ref, a1_ref, a2_ref, w0_ref, a0_ref, kk_ref_in, ka_ref,
                    r_ref, v_ref, g_out_ref, kk_ref, lw0_ref, lw1_ref, b0_ref, b1_ref, kd0_ref, kd1_ref):
    i = pl.program_id(1)
    nt = pl.num_programs(1)
    tm, d = xc_ref.shape[1], xc_ref.shape[2]
    g, sh, sc = g_ref[...], sh_ref[0], sc_ref[0]
    h = _norm_mod(xc_ref[0], g, sh, sc)
    hp = _norm_mod(xp_ref[0], g, sh, sc)[7:8]
    hn = _norm_mod(xn_ref[0], g, sh, sc)[0:1]
    hp = jnp.where(i > 0, hp, 0.0)
    hn = jnp.where(i < nt - 1, hn, 0.0)
    rows = lax.broadcasted_iota(jnp.int32, (tm, d), 0)
    hm1 = jnp.where(rows == 0, hp, pltpu.roll(h, 1, 0))
    hp1 = jnp.where(rows == tm - 1, hn, pltpu.roll(h, tm - 1, 0))
    xx = 0.5 * (hm1 + hp1) - h

    def lerp(n):
        return (h + xx * mu_ref[n:n + 1, :]).astype(BF16)

    r = _dot(lerp(0), wr_ref[...])
    k = _dot(lerp(2), wk_ref[...])
    v = _dot(lerp(3), wv_ref[...])
    r_ref[0] = r.astype(BF16)
    v_ref[0] = v.astype(BF16)
    g_out_ref[0] = _dot(lerp(5), wg_ref[...]).astype(BF16)
    kx = k * kk_ref_in[...]
    ones = _head_ones(LANES, RWKV_HEAD)
    parts = []
    for c in range(d // LANES):
        sl = slice(c * LANES, (c + 1) * LANES)
        q = kx[:, sl]
        parts.append(q * lax.rsqrt(_head_sum(q * q, ones) + 1e-12))
    kk = parts[0] if len(parts) == 1 else jnp.concatenate(parts, axis=1)
    kk_ref[0] = kk.astype(BF16)
    tw =jnp.tanh(_dot(lerp(1), w1_ref[...])).astype(BF16)
    ta = _dot(lerp(4), a1_ref[...]).astype(BF16)
    for dr, (lw_ref, b_ref, kd_ref) in enumerate(((lw0_ref, b0_ref, kd0_ref), (lw1_ref, b1_ref, kd1_ref))):
        z = w0_ref[dr:dr + 1, :] + _dot(tw, w2_ref[dr])
        sp = jnp.maximum(-z, 0.0) + jnp.log(1.0 + jnp.exp(-jnp.abs(z)))
        lw_ref[0] = -jnp.exp(-sp - 0.5)
        a = _sigmoid(a0_ref[dr:dr + 1, :] + _dot(ta, a2_ref[dr]))
        b_ref[0] = (kk * a).astype(BF16)
        kd_ref[0] = (k * (1.0 + (a - 1.0) * ka_ref[...])).astype(BF16)


def _rwkv_feat(x, g, shift, scale, p, *, tm):
    bsz, t, d = x.shape
    assert t % 8 == 0 and tm % 8 == 0
    r8 = tm // 8
    n8 = t // 8
    row = pl.BlockSpec((1, tm, d), lambda b, i: (b, i, 0))
    prev = pl.BlockSpec((1, 8, d), lambda b, i: (b, jnp.maximum(i * r8 - 1, 0), 0))
    nxt = pl.BlockSpec((1, 8, d), lambda b, i: (b, jnp.minimum((i + 1) * r8, n8 - 1), 0))
    lora = p["w1"].shape[-1]
    in_specs = [prev, row, nxt, _const_spec((1, d)), _mod_spec(shift), _mod_spec(scale), _const_spec((6, d)),
                _const_spec((d, d)), _const_spec((d, d)), _const_spec((d, d)), _const_spec((d, d)),
                _const_spec((d, lora)), _const_spec((2, lora, d)), _const_spec((d, lora)), _const_spec((2, lora, d)),
                _const_spec((2, d)), _const_spec((2, d)), _const_spec((1, d)), _const_spec((1, d))]
    return pl.pallas_call(
        _rwkv_feat_body,
        grid=(bsz, t // tm),
        in_specs=in_specs,
        out_specs=[row] * 10,
        out_shape=[jax.ShapeDtypeStruct((bsz, t, d), dt) for dt in [BF16] * 4 + [F32] * 2 + [BF16] * 4],
        compiler_params=_cparams("parallel", "parallel"),
        name="rwkv_feat",
    )(x, x, x, g, shift, scale, p["mu"], p["w_r"], p["w_k"], p["w_v"], p["w_g"],
      p["w1"], p["w2"], p["a1"], p["a2"], p["w0"], p["a0"], p["k_k"], p["k_a"])


def _scan_body(lw_ref, kk_ref, b_ref, kd_ref, v_ref, r_ref, s0_ref, o_ref, s1_ref, h_ref, *, reverse, gl):
    c = pl.program_id(1)
    nc = pl.num_programs(1)
    nb, cs_len, d = lw_ref.shape
    n = RWKV_HEAD
    assert cs_len == n, "block-diagonal packing assumes chunk == head size"
    hp = gl // n

    @pl.when(c == 0)
    def _():
        h_ref[...] = s0_ref[...]

    ri = lax.broadcasted_iota(jnp.int32, (cs_len, cs_len), 0)
    ci = lax.broadcasted_iota(jnp.int32, (cs_len, cs_len), 1)
    tri = jnp.where((ci >= ri) if reverse else (ci <= ri), 1.0, 0.0).astype(BF16)
    kkt, kh, bh, rt, kb, bb, vv, g_tot = ([] for _ in range(8))
    for bi in range(nb):
        lw = lw_ref[bi]
        lw_hi = lw.astype(BF16)
        rem = lw - lw_hi.astype(F32)
        lw_mid = rem.astype(BF16)
        lw_lo = (rem - lw_mid.astype(F32)).astype(BF16)
        cum = _dot(tri, lw_hi) + _dot(tri, lw_mid) + _dot(tri, lw_lo)
        tot = cum[0:1] if reverse else cum[cs_len - 1:cs_len]
        g_inv = jnp.exp(-cum)
        g_tot.append(jnp.exp(tot))
        kkt.append(kk_ref[bi].astype(F32) * jnp.exp(cum - lw))
        kh.append(kd_ref[bi].astype(F32) * g_inv)
        bh.append(b_ref[bi].astype(F32) * g_inv)
        rt.append(r_ref[bi].astype(F32) * jnp.exp(cum))
        kb.append(kh[bi] * g_tot[bi])
        bb.append(bh[bi] * g_tot[bi])
        vv.append(v_ref[bi].astype(F32))

    bdmask = _iota_div((gl, gl), 0, n) == _iota_div((gl, gl), 1, n)
    i_s = lax.broadcasted_iota(jnp.int32, (cs_len, gl), 0)
    j_s = _iota_mod((cs_len, gl), 1, n)
    strict = (j_s > i_s) if reverse else (j_s < i_s)
    incl = (j_s >= i_s) if reverse else (j_s <= i_s)
    eye = j_s == i_s
    shift = INV_BLOCK.bit_length() - 1
    dblock = lax.shift_right_logical(i_s, shift) == lax.shift_right_logical(j_s, shift)
    lane_head = _iota_div((n, gl), 1, n)

    def bd(y):
        return jnp.where(bdmask, jnp.concatenate([y] * hp, axis=0), 0.0).astype(BF16)

    def mm(a, bm):
        return _dot(a.astype(BF16), bm)

    def mm_t(a, bm):
        return lax.dot_general(a.astype(BF16), bm, (((1,), (1,)), ((), ())), preferred_element_type=F32)

    def collapse(m):
        out = jnp.where(lane_head == 0, m[0:n], 0.0)
        for hh in range(1, hp):
            out = out + jnp.where(lane_head == hh, m[hh * n:(hh + 1) * n], 0.0)
        return out

    groups = [(bi, slice(g * gl, (g + 1) * gl)) for bi in range(nb) for g in range(d // gl)]
    cat0 = functools.partial(jnp.concatenate, axis=0)
    lhs = [cat0([kkt[bi][:, sl], rt[bi][:, sl]]) for bi, sl in groups]
    ab = [mm_t(l, bd(bh[bi][:, sl])) for l, (bi, sl) in zip(lhs, groups)]
    ak = [mm_t(l, bd(kh[bi][:, sl])) for l, (bi, sl) in zip(lhs, groups)]
    a_m = [jnp.where(strict, y[:cs_len], 0.0) for y in ab]
    m2 = [jnp.where(incl, y[cs_len:], 0.0) for y in ab]
    b_m = [jnp.where(strict, y[:cs_len], 0.0) for y in ak]
    m1 = [jnp.where(incl, y[cs_len:], 0.0) for y in ak]
    ident = jnp.where(eye, 1.0, 0.0)
    a_d = [jnp.where(dblock, y, 0.0) for y in a_m]
    xp = [-y for y in a_d]
    tm_ = [ident + y for y in xp]
    xp = [mm(y, bd(y)) for y in xp]
    for _ in range(int(math.log2(INV_BLOCK)) - 2):
        st = [mm(cat0([y, z]), bd(y)) for y, z in zip(xp, tm_)]
        xp = [y[:cs_len] for y in st]
        tm_ = [z + y[cs_len:] for y, z in zip(st, tm_)]
    t_d = [z + mm(z, bd(y)) for y, z in zip(xp, tm_)]
    nn = [mm(z, bd(y - w)) for z, y, w in zip(t_d, a_m, a_d)]
    vm = [ident - y for y in nn]
    for _ in range(int(math.log2(cs_len // INV_BLOCK)) - 1):
        nn = [mm(y, bd(y)) for y in nn]
        vm = [z + mm(z, bd(y)) for y, z in zip(nn, vm)]
    tm_ = [mm(z, bd(y)) for y, z in zip(t_d, vm)]
    wt = [mm(z, bd(kkt[bi][:, sl])) for z, (bi, sl) in zip(tm_, groups)]
    sv = [mm(cat0([y, z]), bd(vv[bi][:, sl])) for y, z, (bi, sl) in zip(b_m, m1, groups)]
    u0 = [mm(z, bd(y[:cs_len])) for y, z in zip(sv, tm_)]
    rbar = [rt[bi][:, sl] - mm(y, bd(z)) for y, z, (bi, sl) in zip(m2, wt, groups)]
    o0 = [y[cs_len:] - mm(z, bd(w)) for y, z, w in zip(sv, m2, u0)]
    bbt = [bb[bi][:, sl].T for bi, sl in groups]
    kbt = [kb[bi][:, sl].T for bi, sl in groups]
    pc = [mm(y, z.astype(BF16)) for y, z in zip(bbt, wt)]
    gc = [mm(jnp.concatenate([y, -z], axis=1), cat0([vv[bi][:, sl], w]).astype(BF16))
          for y, z, w, (bi, sl) in zip(kbt, bbt, u0, groups)]
    p_s = [jnp.where(eye, g_tot[bi][:, sl], 0.0) - collapse(y) for y, (bi, sl) in zip(pc, groups)]
    g_s = [collapse(y) for y in gc]
    so = [mm(cat0([y, z]), bd(h_ref[bi, :, sl])) for y, z, (bi, sl) in zip(rbar, p_s, groups)]
    for y, z, w, (bi, sl) in zip(so, o0, g_s, groups):
        o_ref[bi, :, sl] = y[:cs_len] + z
        h_ref[bi, :, sl] = y[cs_len:] + w

    @pl.when(c == nc - 1)
    def _():
        s1_ref[...] = h_ref[...]


def _rwkv_scan(lw, kk, b, kd, v, r, state, *, reverse):
    bsz, t, d = lw.shape
    cs = SCAN_CHUNK
    nc = t // cs
    gl = min(d, 256)
    nb = next(n for n in (4, 2, 1) if bsz % n == 0)
    if reverse:
        row = pl.BlockSpec((nb, cs, d), lambda bi, c: (bi, nc - 1 - c, 0))
    else:
        row = pl.BlockSpec((nb, cs, d), lambda bi, c: (bi, c, 0))
    st = pl.BlockSpec((nb, RWKV_HEAD, d), lambda bi, c: (bi, 0, 0))
    return pl.pallas_call(
        functools.partial(_scan_body, reverse=reverse, gl=gl),
        grid=(bsz // nb, nc),
        in_specs=[row] * 6 + [st],
        out_specs=[row, st],
        out_shape=[jax.ShapeDtypeStruct((bsz, t, d), F32), jax.ShapeDtypeStruct((bsz, RWKV_HEAD, d), F32)],
        scratch_shapes=[pltpu.VMEM((nb, RWKV_HEAD, d), F32)],
        compiler_params=_cparams("parallel", "arbitrary"),
        name="rwkv_scan_rev" if reverse else "rwkv_scan_fwd",
    )(lw, kk, b, kd, v, r, state)


def _rwkv_out_body(o0_ref, o1_ref, r_ref, kd0_ref, kd1_ref, v_ref, g_ref, x_ref, rk_ref, lg_ref, lb_ref, w_ref,
                   gate_ref, out_ref, y_ref):
    d = x_ref.shape[-1]
    ones = _head_ones(LANES, RWKV_HEAD)
    inv_n = 1.0 / RWKV_HEAD
    for c in range(d // LANES):
        sl = slice(c * LANES, (c + 1) * LANES)
        o = o0_ref[0, :, sl] + o1_ref[0, :, sl]
        mean = _head_sum(o, ones) * inv_n
        cen = o - mean
        var = _head_sum(cen * cen, ones) * inv_n
        on = cen * lax.rsqrt(var + GN_EPS) * lg_ref[:, sl] + lb_ref[:, sl]
        ksum = kd0_ref[0, :, sl].astype(F32) + kd1_ref[0, :, sl].astype(F32)
        bonus = _head_sum(r_ref[0, :, sl].astype(F32) * ksum * rk_ref[:, sl], ones) * v_ref[0, :, sl].astype(F32)
        y_ref[:, sl] = ((on + bonus) * _silu(g_ref[0, :, sl].astype(F32))).astype(BF16)
    out_ref[0] = x_ref[0] + gate_ref[0] * _dot(y_ref[...], w_ref[...])


def _rwkv_out(o0, o1, r, kd0, kd1, v, g, x, r_k, ln_g, ln_b, w_o, gate, *, tm):
    bsz, t, d = x.shape
    row = pl.BlockSpec((1, tm, d), lambda b, i: (b, i, 0))
    return pl.pallas_call(
        _rwkv_out_body,
        grid=(bsz, t // tm),
        in_specs=[row] * 8 + [_const_spec((1, d))] * 3 + [_const_spec((d, d)), _mod_spec(gate)],
        out_specs=row,
        out_shape=jax.ShapeDtypeStruct((bsz, t, d), F32),
        scratch_shapes=[pltpu.VMEM((tm, d), BF16)],
        compiler_params=_cparams("parallel", "parallel"),
        name="rwkv_out",
    )(o0, o1, r, kd0, kd1, v, g, x, r_k, ln_g, ln_b, w_o, gate)


def _head_norm(x, g):
    ms = jnp.mean(x * x, axis=-1, keepdims=True)
    return x * lax.rsqrt(ms + NORM_EPS) * g


def _rope(x, cos, sin_signed):
    lane = _iota_mod(x.shape, 1, ATTN_HEAD // 2)
    swapped = jnp.where(lane < ATTN_HEAD // 4, pltpu.roll(x, ATTN_HEAD - ATTN_HEAD // 4, 1),
                        pltpu.roll(x, ATTN_HEAD // 4, 1))
    return x * cos + swapped * sin_signed


def _attn_in_body(x_ref, g_ref, sh_ref, sc_ref, w_ref, qg_ref, kg_ref, cos_ref, sin_ref, *outs,
                  qw, kvw, with_q, rope):
    h = _norm_mod(x_ref[0], g_ref[...], sh_ref[0], sc_ref[0]).astype(BF16)
    hd = ATTN_HEAD
    if with_q:
        q_ref, k_ref, v_ref, gt_ref = outs
        base_k = qw
    else:
        k_ref, v_ref = outs
        base_k = 0
    cos, sin = cos_ref[...], sin_ref[...]
    tn = 2 * hd
    if with_q:
        for n in range(qw // tn):
            qq = _dot(h, w_ref[:, n * tn:(n + 1) * tn])
            for half in range(2):
                q = _head_norm(qq[:, half * hd:(half + 1) * hd], qg_ref[...])
                q_ref[0, :, n * tn + half * hd:n * tn + (half + 1) * hd] = (
                    _rope(q, cos, sin) * (hd ** -0.5)).astype(BF16)
    for n in range(kvw // tn):
        kk2 = _dot(h, w_ref[:, base_k + n * tn:base_k + (n + 1) * tn])
        for half in range(2):
            k = _head_norm(kk2[:, half * hd:(half + 1) * hd], kg_ref[...])
            if rope:
                k = _rope(k, cos, sin)
            k_ref[0, :, n * tn + half * hd:n * tn + (half + 1) * hd] = k.astype(BF16)
    for n in range(kvw // tn):
        v_ref[0, :, n * tn:(n + 1) * tn] = _dot(
            h, w_ref[:, base_k + kvw + n * tn:base_k + kvw + (n + 1) * tn]).astype(BF16)
    if with_q:
        for n in range(qw // tn):
            gt = _dot(h, w_ref[:, qw + 2 * kvw + n * tn:qw + 2 * kvw + (n + 1) * tn])
            gt_ref[0, :, n * tn:(n + 1) * tn] = _silu(gt).astype(BF16)


def _attn_in(x, g, shift, scale, w, q_g, k_g, cos, sin, *, qw, kvw, with_q, rope, tm):
    bsz, t, d = x.shape
    row = pl.BlockSpec((1, tm, d), lambda b, i: (b, i, 0))
    tab = pl.BlockSpec((tm, ATTN_HEAD), lambda b, i: (i, 0))
    nw = w.shape[1]
    outs = []
    shapes = []
    if with_q:
        outs.append(pl.BlockSpec((1, tm, qw), lambda b, i: (b, i, 0)))
        shapes.append(jax.ShapeDtypeStruct((bsz, t, qw), BF16))
    for _ in range(2):
        outs.append(pl.BlockSpec((1, tm, kvw), lambda b, i: (b, i, 0)))
        shapes.append(jax.ShapeDtypeStruct((bsz, t, kvw), BF16))
    if with_q:
        outs.append(pl.BlockSpec((1, tm, qw), lambda b, i: (b, i, 0)))
        shapes.append(jax.ShapeDtypeStruct((bsz, t, qw), BF16))
    return pl.pallas_call(
        functools.partial(_attn_in_body, qw=qw, kvw=kvw, with_q=with_q, rope=rope),
        grid=(bsz, t // tm),
        in_specs=[row, _const_spec((1, d)), _mod_spec(shift), _mod_spec(scale), _const_spec((d, nw)),
                  _const_spec((1, ATTN_HEAD)), _const_spec((1, ATTN_HEAD)), tab, tab],
        out_specs=outs,
        out_shape=shapes,
        compiler_params=_cparams("parallel", "parallel"),
        name="attn_in" if with_q else "attn_in_ctx",
    )(x, g, shift, scale, w, q_g, k_g, cos, sin)


def _flash_body(q_ref, kc_ref, k_ref, vc_ref, v_ref, o_ref, *, sub):
    k = jnp.concatenate([kc_ref[0], k_ref[0]], axis=0)
    v = jnp.concatenate([vc_ref[0], v_ref[0]], axis=0)
    vt = v.astype(F32).T.astype(BF16)
    tq = q_ref.shape[1]
    units = [(slice(r, r + sub), slice(gq * ATTN_HEAD, (gq + 1) * ATTN_HEAD))
             for r in range(0, tq, sub) for gq in range(ATTN_GROUP)]
    nt = (((1,), (1,)), ((), ()))
    st = [lax.dot_general(k, q_ref[0, rs, sl], nt, preferred_element_type=F32) for rs, sl in units]
    m = [jnp.max(s, axis=0, keepdims=True) for s in st]
    p = [jnp.exp(s - mx) for s, mx in zip(st, m)]
    den = [jnp.sum(y, axis=0, keepdims=True) for y in p]
    ot = [_dot(vt, y.astype(BF16)) for y in p]
    for (rs, sl), y, dn in zip(units, ot, den):
        o_ref[0, rs, sl] = (y / dn).T.astype(BF16)


def _flash(q, kc, k, vc, v, *, tq):
    bsz, t, qw = q.shape
    lc = kc.shape[1]
    hkv = k.shape[2] // ATTN_HEAD
    gw = ATTN_GROUP * ATTN_HEAD
    qspec = pl.BlockSpec((1, tq, gw), lambda b, hh, i: (b, i, hh))
    cspec = pl.BlockSpec((1, lc, ATTN_HEAD), lambda b, hh, i: (b, 0, hh))
    kspec = pl.BlockSpec((1, t, ATTN_HEAD), lambda b, hh, i: (b, 0, hh))
    return pl.pallas_call(
        functools.partial(_flash_body, sub=min(tq, 256)),
        grid=(bsz, hkv, t // tq),
        in_specs=[qspec, cspec, kspec, cspec, kspec],
        out_specs=qspec,
        out_shape=jax.ShapeDtypeStruct((bsz, t, qw), BF16),
        compiler_params=_cparams("parallel", "parallel", "arbitrary"),
        name="flash_gqa",
    )(q, kc, k, vc, v)


def _gated_out_body(o_ref, sg_ref, x_ref, w_ref, gate_ref, out_ref):
    u = (o_ref[0].astype(F32) * sg_ref[0].astype(F32)).astype(BF16)
    out_ref[0] = x_ref[0] + gate_ref[0] * _dot(u, w_ref[...])


def _gated_out(o, sg, x, w, gate, *, tm):
    bsz, t, d = x.shape
    kdim = o.shape[-1]
    row = pl.BlockSpec((1, tm, d), lambda b, i: (b, i, 0))
    wide = pl.BlockSpec((1, tm, kdim), lambda b, i: (b, i, 0))
    return pl.pallas_call(
        _gated_out_body,
        grid=(bsz, t // tm),
        in_specs=[wide, wide, row, _const_spec((kdim, d)), _mod_spec(gate)],
        out_specs=row,
        out_shape=jax.ShapeDtypeStruct((bsz, t, d), F32),
        compiler_params=_cparams("parallel", "parallel"),
        name="attn_out",
    )(o, sg, x, w, gate)


def _final_norm_body(x_ref, g_ref, o_ref):
    x = x_ref[0]
    ms = jnp.mean(x * x, axis=-1, keepdims=True)
    o_ref[0] = x * lax.rsqrt(ms + NORM_EPS) * g_ref[...]


def _final_norm(x, g, *, tm):
    bsz, t, d = x.shape
    row = pl.BlockSpec((1, tm, d), lambda b, i: (b, i, 0))
    return pl.pallas_call(
        _final_norm_body,
        grid=(bsz, t // tm),
        in_specs=[row, _const_spec((1, d))],
        out_specs=row,
        out_shape=jax.ShapeDtypeStruct((bsz, t, d), F32),
        compiler_params=_cparams("parallel", "parallel"),
        name="final_norm",
    )(x, g)


def _rope_tables(t):
    axis_dim = ATTN_HEAD // 2
    rows = t // GRID_W
    row = jnp.repeat(jnp.arange(rows), GRID_W).astype(F32)
    col = (jnp.arange(rows * GRID_W) % GRID_W).astype(F32)
    inv = 1.0 / (ROPE_THETA ** (jnp.arange(0, axis_dim, 2, dtype=F32) / axis_dim))
    ar, ac = row[:, None] * inv, col[:, None] * inv
    cos = jnp.concatenate([jnp.cos(ar), jnp.cos(ar), jnp.cos(ac), jnp.cos(ac)], axis=-1)
    sin = jnp.concatenate([-jnp.sin(ar), jnp.sin(ar), -jnp.sin(ac), jnp.sin(ac)], axis=-1)
    return cos, sin


def _tile(t, want):
    tm = min(t, want)
    assert t % tm == 0
    return tm


def kernel(x, c, ctx, c_ctx, norm_g, mod_w, mod_b, conv_w_in, conv_dw, conv_db, conv_ln_g, conv_ln_b, conv_w_out, rwkv_mu, rwkv_w_r, rwkv_w_k, rwkv_w_v, rwkv_w_g, rwkv_w0, rwkv_w1, rwkv_w2, rwkv_a0, rwkv_a1, rwkv_a2, rwkv_k_k, rwkv_k_a, rwkv_r_k, rwkv_ln_g, rwkv_ln_b, rwkv_w_o, attn_w_in, attn_q_g, attn_k_g, attn_w_out, final_g):
    bsz, t, d = x.shape
    lc = ctx.shape[1]
    depth = mod_w.shape[0]
    qw = (d // 64) * ATTN_HEAD
    kvw = qw // ATTN_GROUP

    pad = (-(bsz + 1)) % 8
    cond = jnp.concatenate([c, c_ctx[None, :], jnp.zeros((pad, d), F32)], axis=0)
    mods = _modulation(cond, mod_w, mod_b)
    cos, sin = _rope_tables(t)
    final_row = final_g.reshape(1, d)

    xc = ctx
    for i in range(depth):
        kind, j = i % N_MIXERS, i // N_MIXERS
        ctx_out = any(l % N_MIXERS != 0 for l in range(i + 1, depth))
        ctx_in = ctx_out or kind != 0
        last = i == depth - 1
        m = mods[i]
        shift, scale, gate = (m[:bsz, None, k * d:(k + 1) * d] for k in range(3))
        shift_c, scale_c, gate_c = (m[bsz:bsz + 1, None, k * d:(k + 1) * d] for k in range(3))
        g = norm_g[i].reshape(1, d)
        if kind == 0:
            w3 = conv_w_in[j].reshape(d, 3, d).transpose(1, 0, 2).astype(BF16)
            w_out = conv_w_out[j].astype(BF16)
            cp = (conv_dw[j], conv_db[j].reshape(1, d), conv_ln_g[j].reshape(1, d), conv_ln_b[j].reshape(1, d), w_out)
            x_new = _conv_layer(x, g, shift, scale, w3, *cp, gate, final_row if last else None, tq=_tile(t, 256))
            if ctx_out:
                xc = _conv_layer(xc, g, shift_c, scale_c, w3, *cp, gate_c, None, tq=_tile(lc, 256))
            x = x_new
        elif kind == 1:
            lora = rwkv_w1.shape[-1]
            zero = jnp.zeros((lora, d), F32)

            def padded(w2):
                return jnp.stack([jnp.concatenate([w2[0], zero], 0), jnp.concatenate([zero, w2[1]], 0)]).astype(BF16)

            p = dict(
                mu=rwkv_mu[j], w_r=rwkv_w_r[j].astype(BF16), w_k=rwkv_w_k[j].astype(BF16),
                w_v=rwkv_w_v[j].astype(BF16), w_g=rwkv_w_g[j].astype(BF16),
                w1=jnp.concatenate([rwkv_w1[j, 0], rwkv_w1[j, 1]], axis=-1).astype(BF16), w2=padded(rwkv_w2[j]),
                a1=jnp.concatenate([rwkv_a1[j, 0], rwkv_a1[j, 1]], axis=-1).astype(BF16), a2=padded(rwkv_a2[j]),
                w0=rwkv_w0[j], a0=rwkv_a0[j], k_k=rwkv_k_k[j].reshape(1, d), k_a=rwkv_k_a[j].reshape(1, d))
            p["w1"] = p["w1"].reshape(d, 2 * lora)
            fl = _rwkv_feat(x, g, shift, scale, p, tm=_tile(t, 256))
            fc = _rwkv_feat(xc, g, shift_c, scale_c, p, tm=_tile(lc, 256))
            state0 = jnp.zeros((bsz, RWKV_HEAD, d), F32)
            o_l, o_c = [], []
            for dr, rev in ((0, False), (1, True)):
                oc_d, st = _rwkv_scan(fc[4 + dr], fc[3], fc[6 + dr], fc[8 + dr], fc[1], fc[0], state0, reverse=rev)
                ol_d, _ = _rwkv_scan(fl[4 + dr], fl[3], fl[6 + dr], fl[8 + dr], fl[1], fl[0], st, reverse=rev)
                o_l.append(ol_d)
                o_c.append(oc_d)
            ro = (rwkv_r_k[j].reshape(1, d), rwkv_ln_g[j].reshape(1, d), rwkv_ln_b[j].reshape(1, d),
                  rwkv_w_o[j].astype(BF16))
            x_new = _rwkv_out(o_l[0], o_l[1], fl[0], fl[8], fl[9], fl[1], fl[2], x, *ro, gate, tm=_tile(t, 256))
            if ctx_out:
                xc = _rwkv_out(o_c[0], o_c[1], fc[0], fc[8], fc[9], fc[1], fc[2], xc, *ro, gate_c, tm=_tile(lc, 256))
            x = x_new
        else:
            w_in = attn_w_in[j].astype(BF16)
            qg, kg = attn_q_g[j].reshape(1, ATTN_HEAD), attn_k_g[j].reshape(1, ATTN_HEAD)
            q, k, v, sg = _attn_in(x, g, shift, scale, w_in, qg, kg, cos, sin, qw=qw, kvw=kvw, with_q=True,
                                   rope=True, tm=_tile(t, 256))
            if ctx_out:
                raise NotImplementedError("context-stream attention output is not needed at this depth")
            no_rope = jnp.zeros((lc, ATTN_HEAD), F32)
            kc, vc = _attn_in(xc, g, shift_c, scale_c, w_in[:, qw:qw + 2 * kvw], qg, kg, no_rope, no_rope,
                              qw=qw, kvw=kvw, with_q=False, rope=False, tm=_tile(lc, 256))
            o = _flash(q, kc, k, vc, v, tq=_tile(t, 1024))
            x = _gated_out(o, sg, x, attn_w_out[j].astype(BF16), gate, tm=_tile(t, 512))
        if last and kind != 0:
            x = _final_norm(x, final_row, tm=_tile(t, 256))
    return x
```

```python
import functools
import math

import jax
import jax.numpy as jnp
from jax import lax
from jax.experimental import pallas as pl
from jax.experimental.pallas import tpu as pltpu

N_MIXERS = 3
RWKV_HEAD = 64
ATTN_HEAD = 128
ATTN_GROUP = 2
GRID_W = 64
ROPE_THETA = 10000.0
NORM_EPS = 1e-6
LN_EPS = 1e-5
GN_EPS = 64e-5

V7X_VMEM_BYTES = 64 * 1024 * 1024
VMEM_LIMIT_BYTES = V7X_VMEM_BYTES - 8 * 1024 * 1024
LANES = 128
SUBLANES = 8
INV_BLOCK = 16
SCAN_CHUNK = 64
HALO = 16

F32 = jnp.float32
BF16 = jnp.bfloat16
HI = lax.Precision.HIGHEST


def _cparams(*sem):
    return pltpu.CompilerParams(dimension_semantics=sem, vmem_limit_bytes=VMEM_LIMIT_BYTES)


def _const_spec(shape):
    nd = len(shape)
    return pl.BlockSpec(shape, lambda *_: (0,) * nd, pipeline_mode=pl.Buffered(1))


def _mod_spec(arr):
    nb, _, d = arr.shape
    if nb == 1:
        return pl.BlockSpec((1, 1, d), lambda b, i: (0, 0, 0))
    return pl.BlockSpec((1, 1, d), lambda b, i: (b, 0, 0))


def _sigmoid(x):
    return 1.0 / (1.0 + jnp.exp(-x))


def _silu(x):
    return x * _sigmoid(x)


def _norm_mod(x, g, shift, scale):
    ms = jnp.mean(x * x, axis=-1, keepdims=True)
    n = x * lax.rsqrt(ms + NORM_EPS) * g
    return n * (1.0 + scale) + shift


def _dot(a, b):
    return jnp.dot(a, b, preferred_element_type=F32)


def _dot_hi(a, b):
    return jnp.dot(a, b, preferred_element_type=F32, precision=HI)


def _iota_div(shape, axis, n):
    assert n & (n - 1) == 0
    return lax.shift_right_logical(lax.broadcasted_iota(jnp.int32, shape, axis), n.bit_length() - 1)


def _iota_mod(shape, axis, n):
    assert n & (n - 1) == 0
    return lax.broadcasted_iota(jnp.int32, shape, axis) & (n - 1)


def _head_ones(width, head):
    r = _iota_div((width, width), 0, head)
    c = _iota_div((width, width), 1, head)
    return jnp.where(r == c, 1.0, 0.0).astype(BF16)


def _head_sum(x, ones):
    hi = x.astype(BF16)
    lo = (x - hi.astype(F32)).astype(BF16)
    rows = x.shape[0]
    s = _dot(jnp.concatenate([hi, lo], axis=0), ones)
    return s[:rows] + s[rows:]


def _modulation_body(c_ref, w_ref, b_ref, o_ref):
    s = _silu(c_ref[...])
    o_ref[0] = _dot_hi(s, w_ref[0]) + b_ref[0]


def _modulation(cond, mod_w, mod_b):
    depth, d, d3 = mod_w.shape
    rows = cond.shape[0]
    return pl.pallas_call(
        _modulation_body,
        grid=(depth, d3 // d),
        in_specs=[
            pl.BlockSpec((rows, d), lambda i, j: (0, 0)),
            pl.BlockSpec((1, d, d), lambda i, j: (i, 0, j)),
            pl.BlockSpec((1, 1, d), lambda i, j: (i, 0, j)),
        ],
        out_specs=pl.BlockSpec((1, rows, d), lambda i, j: (i, 0, j)),
        out_shape=jax.ShapeDtypeStruct((depth, rows, d3), F32),
        compiler_params=_cparams("parallel", "parallel"),
        name="modulation",
    )(cond, mod_w, mod_b.reshape(depth, 1, d3))


def _conv_layer_body(xp_ref, xc_ref, xn_ref, g_ref, sh_ref, sc_ref, w3_ref, dw_ref, db_ref, lg_ref, lb_ref,
                     w_ref, gate_ref, *rest, width, final, tn):
    if final:
        fg_ref, o_ref, acc_ref, sg_ref = rest
    else:
        o_ref, acc_ref, sg_ref = rest
    i = pl.program_id(1)
    nt = pl.num_programs(1)
    tq, d = xc_ref.shape[1], xc_ref.shape[2]
    g, sh, sc = g_ref[...], sh_ref[0], sc_ref[0]
    xw = jnp.concatenate([xp_ref[0], xc_ref[0], xn_ref[0]], axis=0)
    h = _norm_mod(xw, g, sh, sc).astype(BF16)
    hc = h[HALO:HALO + tq]
    rows = lax.broadcasted_iota(jnp.int32, (tq + 2 * HALO, tn), 0)
    inside = ((rows >= HALO) | (i > 0)) & ((rows < HALO + tq) | (i < nt - 1))
    off = HALO - width // 2

    def project(n):
        cols = slice(n * tn, (n + 1) * tn)
        a = _dot(h, w3_ref[0, :, cols])
        b = _dot(h, w3_ref[1, :, cols])
        gt = _dot(hc, w3_ref[2, :, cols])
        return jnp.where(inside, a * _sigmoid(b), 0.0), gt

    def conv(n, y, gt):
        cols = slice(n * tn, (n + 1) * tn)
        acc = db_ref[:, cols]
        for res in range(SUBLANES):
            part = None
            for k in range(width):
                if (off + k) % SUBLANES != res:
                    continue
                base = off + k - res
                term = y[base:base + tq + SUBLANES] * dw_ref[k:k + 1, cols]
                part = term if part is None else part + term
            if part is not None:
                acc = acc + part[res:res + tq]
        acc_ref[:, cols] = acc
        sg_ref[:, cols] = _silu(gt)

    nblk = d // tn
    nxt = project(0)
    for n in range(nblk):
        cur = nxt
        if n + 1 < nblk:
            nxt = project(n + 1)
        conv(n, *cur)
    acc = acc_ref[...]
    mean = jnp.mean(acc, axis=-1, keepdims=True)
    cen = acc - mean
    var = jnp.mean(cen * cen, axis=-1, keepdims=True)
    yn = cen * lax.rsqrt(var + LN_EPS) * lg_ref[...] + lb_ref[...]
    u = (_silu(yn) * sg_ref[...]).astype(BF16)
    out = xc_ref[0] + gate_ref[0] * _dot(u, w_ref[...])
    if final:
        ms = jnp.mean(out * out, axis=-1, keepdims=True)
        out = out * lax.rsqrt(ms + NORM_EPS) * fg_ref[...]
    o_ref[0] = out


def _conv_layer(x, g, shift, scale, w3, dw, db, ln_g, ln_b, w_out, gate, final_g, *, tq):
    bsz, t, d = x.shape
    width = dw.shape[0]
    assert width // 2 <= HALO and t % HALO == 0 and tq % HALO == 0
    r = tq // HALO
    nh = t // HALO
    tn = min(d, 256)
    row = pl.BlockSpec((1, tq, d), lambda b, i: (b, i, 0))
    prev = pl.BlockSpec((1, HALO, d), lambda b, i: (b, jnp.maximum(i * r - 1, 0), 0))
    nxt = pl.BlockSpec((1, HALO, d), lambda b, i: (b, jnp.minimum((i + 1) * r, nh - 1), 0))
    final = final_g is not None
    in_specs = [prev, row, nxt, _const_spec((1, d)), _mod_spec(shift), _mod_spec(scale), _const_spec((3, d, d)),
                _const_spec((width, d)), _const_spec((1, d)), _const_spec((1, d)), _const_spec((1, d)),
                _const_spec((d, d)), _mod_spec(gate)]
    args = [x, x, x, g, shift, scale, w3, dw, db, ln_g, ln_b, w_out, gate]
    if final:
        in_specs.append(_const_spec((1, d)))
        args.append(final_g)
    return pl.pallas_call(
        functools.partial(_conv_layer_body, width=width, final=final, tn=tn),
        grid=(bsz, t // tq),
        in_specs=in_specs,
        out_specs=row,
        out_shape=jax.ShapeDtypeStruct((bsz, t, d), F32),
        scratch_shapes=[pltpu.VMEM((tq, d), F32), pltpu.VMEM((tq, d), F32)],
        compiler_params=_cparams("parallel", "parallel"),
        name="conv_layer",
    )(*args)


def _rwkv_feat_body(xp_ref, xc_ref, xn_ref, g_ref, sh_ref, sc_ref, mu_ref, wr_ref, wk_ref, wv_ref, wg_ref,
                    w1_ref, w2_ref, a1_ref, a2_ref, w0_ref, a0_ref, kk_ref_in, ka_ref,
                    r_ref, v_ref, g_out_ref, kk_ref, lw0_ref, lw1_ref, b0_ref, b1_ref, kd0_ref, kd1_ref):
    i = pl.program_id(1)
    nt = pl.num_programs(1)
    tm, d = xc_ref.shape[1], xc_ref.shape[2]
    g, sh, sc = g_ref[...], sh_ref[0], sc_ref[0]
    h = _norm_mod(xc_ref[0], g, sh, sc)
    hp = _norm_mod(xp_ref[0], g, sh, sc)[7:8]
    hn = _norm_mod(xn_ref[0], g, sh, sc)[0:1]
    hp = jnp.where(i > 0, hp, 0.0)
    hn = jnp.where(i < nt - 1, hn, 0.0)
    rows = lax.broadcasted_iota(jnp.int32, (tm, d), 0)
    hm1 = jnp.where(rows == 0, hp, pltpu.roll(h, 1, 0))
    hp1 = jnp.where(rows == tm - 1, hn, pltpu.roll(h, tm - 1, 0))
    xx = 0.5 * (hm1 + hp1) - h

    def lerp(n):
        return (h + xx * mu_ref[n:n + 1, :]).astype(BF16)

    r = _dot(lerp(0), wr_ref[...])
    k = _dot(lerp(2), wk_ref[...])
    v = _dot(lerp(3), wv_ref[...])
    r_ref[0] = r.astype(BF16)
    v_ref[0] = v.astype(BF16)
    g_out_ref[0] = _dot(lerp(5), wg_ref[...]).astype(BF16)
    kx = k * kk_ref_in[...]
    ones = _head_ones(LANES, RWKV_HEAD)
    parts = []
    for c in range(d // LANES):
        sl = slice(c * LANES, (c + 1) * LANES)
        q = kx[:, sl]
        parts.append(q * lax.rsqrt(_head_sum(q * q, ones) + 1e-12))
    kk = parts[0] if len(parts) == 1 else jnp.concatenate(parts, axis=1)
    kk_ref[0] = kk.astype(BF16)
    tw =jnp.tanh(_dot(lerp(1), w1_ref[...])).astype(BF16)
    ta = _dot(lerp(4), a1_ref[...]).astype(BF16)
    for dr, (lw_ref, b_ref, kd_ref) in enumerate(((lw0_ref, b0_ref, kd0_ref), (lw1_ref, b1_ref, kd1_ref))):
        z = w0_ref[dr:dr + 1, :] + _dot(tw, w2_ref[dr])
        sp = jnp.maximum(-z, 0.0) + jnp.log(1.0 + jnp.exp(-jnp.abs(z)))
        lw_ref[0] = -jnp.exp(-sp - 0.5)
        a = _sigmoid(a0_ref[dr:dr + 1, :] + _dot(ta, a2_ref[dr]))
        b_ref[0] = (kk * a).astype(BF16)
        kd_ref[0] = (k * (1.0 + (a - 1.0) * ka_ref[...])).astype(BF16)


def _rwkv_feat(x, g, shift, scale, p, *, tm):
    bsz, t, d = x.shape
    assert t % 8 == 0 and tm % 8 == 0
    r8 = tm // 8
    n8 = t // 8
    row = pl.BlockSpec((1, tm, d), lambda b, i: (b, i, 0))
    prev = pl.BlockSpec((1, 8, d), lambda b, i: (b, jnp.maximum(i * r8 - 1, 0), 0))
    nxt = pl.BlockSpec((1, 8, d), lambda b, i: (b, jnp.minimum((i + 1) * r8, n8 - 1), 0))
    lora = p["w1"].shape[-1]
    in_specs = [prev, row, nxt, _const_spec((1, d)), _mod_spec(shift), _mod_spec(scale), _const_spec((6, d)),
                _const_spec((d, d)), _const_spec((d, d)), _const_spec((d, d)), _const_spec((d, d)),
                _const_spec((d, lora)), _const_spec((2, lora, d)), _const_spec((d, lora)), _const_spec((2, lora, d)),
                _const_spec((2, d)), _const_spec((2, d)), _const_spec((1, d)), _const_spec((1, d))]
    return pl.pallas_call(
        _rwkv_feat_body,
        grid=(bsz, t // tm),
        in_specs=in_specs,
        out_specs=[row] * 10,
        out_shape=[jax.ShapeDtypeStruct((bsz, t, d), dt) for dt in [BF16] * 4 + [F32] * 2 + [BF16] * 4],
        compiler_params=_cparams("parallel", "parallel"),
        name="rwkv_feat",
    )(x, x, x, g, shift, scale, p["mu"], p["w_r"], p["w_k"], p["w_v"], p["w_g"],
      p["w1"], p["w2"], p["a1"], p["a2"], p["w0"], p["a0"], p["k_k"], p["k_a"])


def _scan_body(lw_ref, kk_ref, b_ref, kd_ref, v_ref, r_ref, s0_ref, o_ref, s1_ref, h_ref, *, reverse, gl):
    c = pl.program_id(1)
    nc = pl.num_programs(1)
    nb, cs_len, d = lw_ref.shape
    n = RWKV_HEAD
    assert cs_len == n, "block-diagonal packing assumes chunk == head size"
    hp = gl // n

    @pl.when(c == 0)
    def _():
        h_ref[...] = s0_ref[...]

    ri = lax.broadcasted_iota(jnp.int32, (cs_len, cs_len), 0)
    ci = lax.broadcasted_iota(jnp.int32, (cs_len, cs_len), 1)
    tri = jnp.where((ci >= ri) if reverse else (ci <= ri), 1.0, 0.0).astype(BF16)
    kkt, kh, bh, rt, kb, bb, vv, g_tot = ([] for _ in range(8))
    for bi in range(nb):
        lw = lw_ref[bi]
        lw_hi = lw.astype(BF16)
        rem = lw - lw_hi.astype(F32)
        lw_mid = rem.astype(BF16)
        lw_lo = (rem - lw_mid.astype(F32)).astype(BF16)
        cum = _dot(tri, lw_hi) + _dot(tri, lw_mid) + _dot(tri, lw_lo)
        tot = cum[0:1] if reverse else cum[cs_len - 1:cs_len]
        g_inv = jnp.exp(-cum)
        g_tot.append(jnp.exp(tot))
        kkt.append(kk_ref[bi].astype(F32) * jnp.exp(cum - lw))
        kh.append(kd_ref[bi].astype(F32) * g_inv)
        bh.append(b_ref[bi].astype(F32) * g_inv)
        rt.append(r_ref[bi].astype(F32) * jnp.exp(cum))
        kb.append(kh[bi] * g_tot[bi])
        bb.append(bh[bi] * g_tot[bi])
        vv.append(v_ref[bi].astype(F32))

    bdmask = _iota_div((gl, gl), 0, n) == _iota_div((gl, gl), 1, n)
    i_s = lax.broadcasted_iota(jnp.int32, (cs_len, gl), 0)
    j_s = _iota_mod((cs_len, gl), 1, n)
    strict = (j_s > i_s) if reverse else (j_s < i_s)
    incl = (j_s >= i_s) if reverse else (j_s <= i_s)
    eye = j_s == i_s
    shift = INV_BLOCK.bit_length() - 1
    dblock = lax.shift_right_logical(i_s, shift) == lax.shift_right_logical(j_s, shift)
    lane_head = _iota_div((n, gl), 1, n)

    def bd(y):
        return jnp.where(bdmask, jnp.concatenate([y] * hp, axis=0), 0.0).astype(BF16)

    def mm(a, bm):
        return _dot(a.astype(BF16), bm)

    def mm_t(a, bm):
        return lax.dot_general(a.astype(BF16), bm, (((1,), (1,)), ((), ())), preferred_element_type=F32)

    def collapse(m):
        out = jnp.where(lane_head == 0, m[0:n], 0.0)
        for hh in range(1, hp):
            out = out + jnp.where(lane_head == hh, m[hh * n:(hh + 1) * n], 0.0)
        return out

    groups = [(bi, slice(g * gl, (g + 1) * gl)) for bi in range(nb) for g in range(d // gl)]
    cat0 = functools.partial(jnp.concatenate, axis=0)
    lhs = [cat0([kkt[bi][:, sl], rt[bi][:, sl]]) for bi, sl in groups]
    ab = [mm_t(l, bd(bh[bi][:, sl])) for l, (bi, sl) in zip(lhs, groups)]
    ak = [mm_t(l, bd(kh[bi][:, sl])) for l, (bi, sl) in zip(lhs, groups)]
    a_m = [jnp.where(strict, y[:cs_len], 0.0) for y in ab]
    m2 = [jnp.where(incl, y[cs_len:], 0.0) for y in ab]
    b_m = [jnp.where(strict, y[:cs_len], 0.0) for y in ak]
    m1 = [jnp.where(incl, y[cs_len:], 0.0) for y in ak]
    ident = jnp.where(eye, 1.0, 0.0)
    a_d = [jnp.where(dblock, y, 0.0) for y in a_m]
    xp = [-y for y in a_d]
    tm_ = [ident + y for y in xp]
    xp = [mm(y, bd(y)) for y in xp]
    for _ in range(int(math.log2(INV_BLOCK)) - 2):
        st = [mm(cat0([y, z]), bd(y)) for y, z in zip(xp, tm_)]
        xp = [y[:cs_len] for y in st]
        tm_ = [z + y[cs_len:] for y, z in zip(st, tm_)]
    t_d = [z + mm(z, bd(y)) for y, z in zip(xp, tm_)]
    nn = [mm(z, bd(y - w)) for z, y, w in zip(t_d, a_m, a_d)]
    vm = [ident - y for y in nn]
    for _ in range(int(math.log2(cs_len // INV_BLOCK)) - 1):
        nn = [mm(y, bd(y)) for y in nn]
        vm = [z + mm(z, bd(y)) for y, z in zip(nn, vm)]
    tm_ = [mm(z, bd(y)) for y, z in zip(t_d, vm)]
    wt = [mm(z, bd(kkt[bi][:, sl])) for z, (bi, sl) in zip(tm_, groups)]
    sv = [mm(cat0([y, z]), bd(vv[bi][:, sl])) for y, z, (bi, sl) in zip(b_m, m1, groups)]
    u0 = [mm(z, bd(y[:cs_len])) for y, z in zip(sv, tm_)]
    rbar = [rt[bi][:, sl] - mm(y, bd(z)) for y, z, (bi, sl) in zip(m2, wt, groups)]
    o0 = [y[cs_len:] - mm(z, bd(w)) for y, z, w in zip(sv, m2, u0)]
    bbt = [bb[bi][:, sl].T for bi, sl in groups]
    kbt = [kb[bi][:, sl].T for bi, sl in groups]
    pc = [mm(y, z.astype(BF16)) for y, z in zip(bbt, wt)]
    gc = [mm(jnp.concatenate([y, -z], axis=1), cat0([vv[bi][:, sl], w]).astype(BF16))
          for y, z, w, (bi, sl) in zip(kbt, bbt, u0, groups)]
    p_s = [jnp.where(eye, g_tot[bi][:, sl], 0.0) - collapse(y) for y, (bi, sl) in zip(pc, groups)]
    g_s = [collapse(y) for y in gc]
    so = [mm(cat0([y, z]), bd(h_ref[bi, :, sl])) for y, z, (bi, sl) in zip(rbar, p_s, groups)]
    for y, z, w, (bi, sl) in zip(so, o0, g_s, groups):
        o_ref[bi, :, sl] = y[:cs_len] + z
        h_ref[bi, :, sl] = y[cs_len:] + w

    @pl.when(c == nc - 1)
    def _():
        s1_ref[...] = h_ref[...]


def _rwkv_scan(lw, kk, b, kd, v, r, state, *, reverse):
    bsz, t, d = lw.shape
    cs = SCAN_CHUNK
    nc = t // cs
    gl = min(d, 256)
    nb = next(n for n in (4, 2, 1) if bsz % n == 0)
    if reverse:
        row = pl.BlockSpec((nb, cs, d), lambda bi, c: (bi, nc - 1 - c, 0))
    else:
        row = pl.BlockSpec((nb, cs, d), lambda bi, c: (bi, c, 0))
    st = pl.BlockSpec((nb, RWKV_HEAD, d), lambda bi, c: (bi, 0, 0))
    return pl.pallas_call(
        functools.partial(_scan_body, reverse=reverse, gl=gl),
        grid=(bsz // nb, nc),
        in_specs=[row] * 6 + [st],
        out_specs=[row, st],
        out_shape=[jax.ShapeDtypeStruct((bsz, t, d), F32), jax.ShapeDtypeStruct((bsz, RWKV_HEAD, d), F32)],
        scratch_shapes=[pltpu.VMEM((nb, RWKV_HEAD, d), F32)],
        compiler_params=_cparams("parallel", "arbitrary"),
        name="rwkv_scan_rev" if reverse else "rwkv_scan_fwd",
    )(lw, kk, b, kd, v, r, state)


def _rwkv_out_body(o0_ref, o1_ref, r_ref, kd0_ref, kd1_ref, v_ref, g_ref, x_ref, rk_ref, lg_ref, lb_ref, w_ref,
                   gate_ref, out_ref, y_ref):
    d = x_ref.shape[-1]
    ones = _head_ones(LANES, RWKV_HEAD)
    inv_n = 1.0 / RWKV_HEAD
    for c in range(d // LANES):
        sl = slice(c * LANES, (c + 1) * LANES)
        o = o0_ref[0, :, sl] + o1_ref[0, :, sl]
        mean = _head_sum(o, ones) * inv_n
        cen = o - mean
        var = _head_sum(cen * cen, ones) * inv_n
        on = cen * lax.rsqrt(var + GN_EPS) * lg_ref[:, sl] + lb_ref[:, sl]
        ksum = kd0_ref[0, :, sl].astype(F32) + kd1_ref[0, :, sl].astype(F32)
        bonus = _head_sum(r_ref[0, :, sl].astype(F32) * ksum * rk_ref[:, sl], ones) * v_ref[0, :, sl].astype(F32)
        y_ref[:, sl] = ((on + bonus) * _silu(g_ref[0, :, sl].astype(F32))).astype(BF16)
    out_ref[0] = x_ref[0] + gate_ref[0] * _dot(y_ref[...], w_ref[...])


def _rwkv_out(o0, o1, r, kd0, kd1, v, g, x, r_k, ln_g, ln_b, w_o, gate, *, tm):
    bsz, t, d = x.shape
    row = pl.BlockSpec((1, tm, d), lambda b, i: (b, i, 0))
    return pl.pallas_call(
        _rwkv_out_body,
        grid=(bsz, t // tm),
        in_specs=[row] * 8 + [_const_spec((1, d))] * 3 + [_const_spec((d, d)), _mod_spec(gate)],
        out_specs=row,
        out_shape=jax.ShapeDtypeStruct((bsz, t, d), F32),
        scratch_shapes=[pltpu.VMEM((tm, d), BF16)],
        compiler_params=_cparams("parallel", "parallel"),
        name="rwkv_out",
    )(o0, o1, r, kd0, kd1, v, g, x, r_k, ln_g, ln_b, w_o, gate)


def _head_norm(x, g):
    ms = jnp.mean(x * x, axis=-1, keepdims=True)
    return x * lax.rsqrt(ms + NORM_EPS) * g


def _rope(x, cos, sin_signed):
    lane = _iota_mod(x.shape, 1, ATTN_HEAD // 2)
    swapped = jnp.where(lane < ATTN_HEAD // 4, pltpu.roll(x, ATTN_HEAD - ATTN_HEAD // 4, 1),
                        pltpu.roll(x, ATTN_HEAD // 4, 1))
    return x * cos + swapped * sin_signed


def _attn_in_body(x_ref, g_ref, sh_ref, sc_ref, w_ref, qg_ref, kg_ref, cos_ref, sin_ref, *outs,
                  qw, kvw, with_q, rope):
    h = _norm_mod(x_ref[0], g_ref[...], sh_ref[0], sc_ref[0]).astype(BF16)
    hd = ATTN_HEAD
    if with_q:
        q_ref, k_ref, v_ref, gt_ref = outs
        base_k = qw
    else:
        k_ref, v_ref = outs
        base_k = 0
    cos, sin = cos_ref[...], sin_ref[...]
    tn = 2 * hd
    if with_q:
        for n in range(qw // tn):
            qq = _dot(h, w_ref[:, n * tn:(n + 1) * tn])
            for half in range(2):
                q = _head_norm(qq[:, half * hd:(half + 1) * hd], qg_ref[...])
                q_ref[0, :, n * tn + half * hd:n * tn + (half + 1) * hd] = (
                    _rope(q, cos, sin) * (hd ** -0.5)).astype(BF16)
    for n in range(kvw // tn):
        kk2 = _dot(h, w_ref[:, base_k + n * tn:base_k + (n + 1) * tn])
        for half in range(2):
            k = _head_norm(kk2[:, half * hd:(half + 1) * hd], kg_ref[...])
            if rope:
                k = _rope(k, cos, sin)
            k_ref[0, :, n * tn + half * hd:n * tn + (half + 1) * hd] = k.astype(BF16)
    for n in range(kvw // tn):
        v_ref[0, :, n * tn:(n + 1) * tn] = _dot(
            h, w_ref[:, base_k + kvw + n * tn:base_k + kvw + (n + 1) * tn]).astype(BF16)
    if with_q:
        for n in range(qw // tn):
            gt = _dot(h, w_ref[:, qw + 2 * kvw + n * tn:qw + 2 * kvw + (n + 1) * tn])
            gt_ref[0, :, n * tn:(n + 1) * tn] = _silu(gt).astype(BF16)


def _attn_in(x, g, shift, scale, w, q_g, k_g, cos, sin, *, qw, kvw, with_q, rope, tm):
    bsz, t, d = x.shape
    row = pl.BlockSpec((1, tm, d), lambda b, i: (b, i, 0))
    tab = pl.BlockSpec((tm, ATTN_HEAD), lambda b, i: (i, 0))
    nw = w.shape[1]
    outs = []
    shapes = []
    if with_q:
        outs.append(pl.BlockSpec((1, tm, qw), lambda b, i: (b, i, 0)))
        shapes.append(jax.ShapeDtypeStruct((bsz, t, qw), BF16))
    for _ in range(2):
        outs.append(pl.BlockSpec((1, tm, kvw), lambda b, i: (b, i, 0)))
        shapes.append(jax.ShapeDtypeStruct((bsz, t, kvw), BF16))
    if with_q:
        outs.append(pl.BlockSpec((1, tm, qw), lambda b, i: (b, i, 0)))
        shapes.append(jax.ShapeDtypeStruct((bsz, t, qw), BF16))
    return pl.pallas_call(
        functools.partial(_attn_in_body, qw=qw, kvw=kvw, with_q=with_q, rope=rope),
        grid=(bsz, t // tm),
        in_specs=[row, _const_spec((1, d)), _mod_spec(shift), _mod_spec(scale), _const_spec((d, nw)),
                  _const_spec((1, ATTN_HEAD)), _const_spec((1, ATTN_HEAD)), tab, tab],
        out_specs=outs,
        out_shape=shapes,
        compiler_params=_cparams("parallel", "parallel"),
        name="attn_in" if with_q else "attn_in_ctx",
    )(x, g, shift, scale, w, q_g, k_g, cos, sin)


def _flash_body(q_ref, kc_ref, k_ref, vc_ref, v_ref, o_ref, *, sub):
    k = jnp.concatenate([kc_ref[0], k_ref[0]], axis=0)
    v = jnp.concatenate([vc_ref[0], v_ref[0]], axis=0)
    vt = v.astype(F32).T.astype(BF16)
    tq = q_ref.shape[1]
    units = [(slice(r, r + sub), slice(gq * ATTN_HEAD, (gq + 1) * ATTN_HEAD))
             for r in range(0, tq, sub) for gq in range(ATTN_GROUP)]
    nt = (((1,), (1,)), ((), ()))
    st = [lax.dot_general(k, q_ref[0, rs, sl], nt, preferred_element_type=F32) for rs, sl in units]
    m = [jnp.max(s, axis=0, keepdims=True) for s in st]
    p = [jnp.exp(s - mx) for s, mx in zip(st, m)]
    den = [jnp.sum(y, axis=0, keepdims=True) for y in p]
    ot = [_dot(vt, y.astype(BF16)) for y in p]
    for (rs, sl), y, dn in zip(units, ot, den):
        o_ref[0, rs, sl] = (y / dn).T.astype(BF16)


def _flash(q, kc, k, vc, v, *, tq):
    bsz, t, qw = q.shape
    lc = kc.shape[1]
    hkv = k.shape[2] // ATTN_HEAD
    gw = ATTN_GROUP * ATTN_HEAD
    qspec = pl.BlockSpec((1, tq, gw), lambda b, hh, i: (b, i, hh))
    cspec = pl.BlockSpec((1, lc, ATTN_HEAD), lambda b, hh, i: (b, 0, hh))
    kspec = pl.BlockSpec((1, t, ATTN_HEAD), lambda b, hh, i: (b, 0, hh))
    return pl.pallas_call(
        functools.partial(_flash_body, sub=min(tq, 256)),
        grid=(bsz, hkv, t // tq),
        in_specs=[qspec, cspec, kspec, cspec, kspec],
        out_specs=qspec,
        out_shape=jax.ShapeDtypeStruct((bsz, t, qw), BF16),
        compiler_params=_cparams("parallel", "parallel", "arbitrary"),
        name="flash_gqa",
    )(q, kc, k, vc, v)


def _gated_out_body(o_ref, sg_ref, x_ref, w_ref, gate_ref, out_ref):
    u = (o_ref[0].astype(F32) * sg_ref[0].astype(F32)).astype(BF16)
    out_ref[0] = x_ref[0] + gate_ref[0] * _dot(u, w_ref[...])


def _gated_out(o, sg, x, w, gate, *, tm):
    bsz, t, d = x.shape
    kdim = o.shape[-1]
    row = pl.BlockSpec((1, tm, d), lambda b, i: (b, i, 0))
    wide = pl.BlockSpec((1, tm, kdim), lambda b, i: (b, i, 0))
    return pl.pallas_call(
        _gated_out_body,
        grid=(bsz, t // tm),
        in_specs=[wide, wide, row, _const_spec((kdim, d)), _mod_spec(gate)],
        out_specs=row,
        out_shape=jax.ShapeDtypeStruct((bsz, t, d), F32),
        compiler_params=_cparams("parallel", "parallel"),
        name="attn_out",
    )(o, sg, x, w, gate)


def _final_norm_body(x_ref, g_ref, o_ref):
    x = x_ref[0]
    ms = jnp.mean(x * x, axis=-1, keepdims=True)
    o_ref[0] = x * lax.rsqrt(ms + NORM_EPS) * g_ref[...]


def _final_norm(x, g, *, tm):
    bsz, t, d = x.shape
    row = pl.BlockSpec((1, tm, d), lambda b, i: (b, i, 0))
    return pl.pallas_call(
        _final_norm_body,
        grid=(bsz, t // tm),
        in_specs=[row, _const_spec((1, d))],
        out_specs=row,
        out_shape=jax.ShapeDtypeStruct((bsz, t, d), F32),
        compiler_params=_cparams("parallel", "parallel"),
        name="final_norm",
    )(x, g)


def _rope_tables(t):
    axis_dim = ATTN_HEAD // 2
    rows = t // GRID_W
    row = jnp.repeat(jnp.arange(rows), GRID_W).astype(F32)
    col = (jnp.arange(rows * GRID_W) % GRID_W).astype(F32)
    inv = 1.0 / (ROPE_THETA ** (jnp.arange(0, axis_dim, 2, dtype=F32) / axis_dim))
    ar, ac = row[:, None] * inv, col[:, None] * inv
    cos = jnp.concatenate([jnp.cos(ar), jnp.cos(ar), jnp.cos(ac), jnp.cos(ac)], axis=-1)
    sin = jnp.concatenate([-jnp.sin(ar), jnp.sin(ar), -jnp.sin(ac), jnp.sin(ac)], axis=-1)
    return cos, sin


def _tile(t, want):
    tm = min(t, want)
    assert t % tm == 0
    return tm


def kernel(x, c, ctx, c_ctx, norm_g, mod_w, mod_b, conv_w_in, conv_dw, conv_db, conv_ln_g, conv_ln_b, conv_w_out, rwkv_mu, rwkv_w_r, rwkv_w_k, rwkv_w_v, rwkv_w_g, rwkv_w0, rwkv_w1, rwkv_w2, rwkv_a0, rwkv_a1, rwkv_a2, rwkv_k_k, rwkv_k_a, rwkv_r_k, rwkv_ln_g, rwkv_ln_b, rwkv_w_o, attn_w_in, attn_q_g, attn_k_g, attn_w_out, final_g):
    bsz, t, d = x.shape
    lc = ctx.shape[1]
    depth = mod_w.shape[0]
    qw = (d // 64) * ATTN_HEAD
    kvw = qw // ATTN_GROUP

    pad = (-(bsz + 1)) % 8
    cond = jnp.concatenate([c, c_ctx[None, :], jnp.zeros((pad, d), F32)], axis=0)
    mods = _modulation(cond, mod_w, mod_b)
    cos, sin = _rope_tables(t)
    final_row = final_g.reshape(1, d)

    xc = ctx
    for i in range(depth):
        kind, j = i % N_MIXERS, i // N_MIXERS
        ctx_out = any(l % N_MIXERS != 0 for l in range(i + 1, depth))
        ctx_in = ctx_out or kind != 0
        last = i == depth - 1
        m = mods[i]
        shift, scale, gate = (m[:bsz, None, k * d:(k + 1) * d] for k in range(3))
        shift_c, scale_c, gate_c = (m[bsz:bsz + 1, None, k * d:(k + 1) * d] for k in range(3))
        g = norm_g[i].reshape(1, d)
        if kind == 0:
            w3 = conv_w_in[j].reshape(d, 3, d).transpose(1, 0, 2).astype(BF16)
            w_out = conv_w_out[j].astype(BF16)
            cp = (conv_dw[j], conv_db[j].reshape(1, d), conv_ln_g[j].reshape(1, d), conv_ln_b[j].reshape(1, d), w_out)
            x_new = _conv_layer(x, g, shift, scale, w3, *cp, gate, final_row if last else None, tq=_tile(t, 512))
            if ctx_out:
                xc = _conv_layer(xc, g, shift_c, scale_c, w3, *cp, gate_c, None, tq=_tile(lc, 256))
            x = x_new
        elif kind == 1:
            lora = rwkv_w1.shape[-1]
            zero = jnp.zeros((lora, d), F32)

            def padded(w2):
                return jnp.stack([jnp.concatenate([w2[0], zero], 0), jnp.concatenate([zero, w2[1]], 0)]).astype(BF16)

            p = dict(
                mu=rwkv_mu[j], w_r=rwkv_w_r[j].astype(BF16), w_k=rwkv_w_k[j].astype(BF16),
                w_v=rwkv_w_v[j].astype(BF16), w_g=rwkv_w_g[j].astype(BF16),
                w1=jnp.concatenate([rwkv_w1[j, 0], rwkv_w1[j, 1]], axis=-1).astype(BF16), w2=padded(rwkv_w2[j]),
                a1=jnp.concatenate([rwkv_a1[j, 0], rwkv_a1[j, 1]], axis=-1).astype(BF16), a2=padded(rwkv_a2[j]),
                w0=rwkv_w0[j], a0=rwkv_a0[j], k_k=rwkv_k_k[j].reshape(1, d), k_a=rwkv_k_a[j].reshape(1, d))
            p["w1"] = p["w1"].reshape(d, 2 * lora)
            fl = _rwkv_feat(x, g, shift, scale, p, tm=_tile(t, 512))
            fc = _rwkv_feat(xc, g, shift_c, scale_c, p, tm=_tile(lc, 256))
            state0 = jnp.zeros((bsz, RWKV_HEAD, d), F32)
            o_l, o_c = [], []
            for dr, rev in ((0, False), (1, True)):
                oc_d, st = _rwkv_scan(fc[4 + dr], fc[3], fc[6 + dr], fc[8 + dr], fc[1], fc[0], state0, reverse=rev)
                ol_d, _ = _rwkv_scan(fl[4 + dr], fl[3], fl[6 + dr], fl[8 + dr], fl[1], fl[0], st, reverse=rev)
                o_l.append(ol_d)
                o_c.append(oc_d)
            ro = (rwkv_r_k[j].reshape(1, d), rwkv_ln_g[j].reshape(1, d), rwkv_ln_b[j].reshape(1, d),
                  rwkv_w_o[j].astype(BF16))
            x_new = _rwkv_out(o_l[0], o_l[1], fl[0], fl[8], fl[9], fl[1], fl[2], x, *ro, gate, tm=_tile(t, 512))
            if ctx_out:
                xc = _rwkv_out(o_c[0], o_c[1], fc[0], fc[8], fc[9], fc[1], fc[2], xc, *ro, gate_c, tm=_tile(lc, 256))
            x = x_new
        else:
            w_in = attn_w_in[j].astype(BF16)
            qg, kg = attn_q_g[j].reshape(1, ATTN_HEAD), attn_k_g[j].reshape(1, ATTN_HEAD)
            q, k, v, sg = _attn_in(x, g, shift, scale, w_in, qg, kg, cos, sin, qw=qw, kvw=kvw, with_q=True,
                                   rope=True, tm=_tile(t, 512))
            if ctx_out:
                raise NotImplementedError("context-stream attention output is not needed at this depth")
            no_rope = jnp.zeros((lc, ATTN_HEAD), F32)
            kc, vc = _attn_in(xc, g, shift_c, scale_c, w_in[:, qw:qw + 2 * kvw], qg, kg, no_rope, no_rope,
                              qw=qw, kvw=kvw, with_q=False, rope=False, tm=_tile(lc, 256))
            o = _flash(q, kc, k, vc, v, tq=_tile(t, 1024))
            x = _gated_out(o, sg, x, attn_w_out[j].astype(BF16), gate, tm=_tile(t, 512))
        if last and kind != 0:
            x = _final_norm(x, final_row, tm=_tile(t, 256))
    return x
```

```python
import functools
import math

import jax
import jax.numpy as jnp
from jax import lax
from jax.experimental import pallas as pl
from jax.experimental.pallas import tpu as pltpu

N_MIXERS = 3
RWKV_HEAD = 64
ATTN_HEAD = 128
ATTN_GROUP = 2
GRID_W = 64
ROPE_THETA = 10000.0
NORM_EPS = 1e-6
LN_EPS = 1e-5
GN_EPS = 64e-5

V7X_VMEM_BYTES = 64 * 1024 * 1024
VMEM_LIMIT_BYTES = V7X_VMEM_BYTES - 8 * 1024 * 1024
LANES = 128
SUBLANES = 8
INV_BLOCK = 16
SCAN_CHUNK = 64
SCAN_BATCH = 4
HALO = 16
ROW_TILE = 512
FLASH_Q_TILE = 2048

F32 = jnp.float32
BF16 = jnp.bfloat16
HI = lax.Precision.HIGHEST


def _cparams(*sem):
    return pltpu.CompilerParams(dimension_semantics=sem, vmem_limit_bytes=VMEM_LIMIT_BYTES)


def _const_spec(shape):
    nd = len(shape)
    return pl.BlockSpec(shape, lambda *_: (0,) * nd, pipeline_mode=pl.Buffered(1))


def _mod_spec(arr):
    nb, _, d = arr.shape
    if nb == 1:
        return pl.BlockSpec((1, 1, d), lambda b, i: (0, 0, 0))
    return pl.BlockSpec((1, 1, d), lambda b, i: (b, 0, 0))


def _sigmoid(x):
    return 1.0 / (1.0 + jnp.exp(-x))


def _silu(x):
    return x * _sigmoid(x)


def _norm_mod(x, g, shift, scale):
    ms = jnp.mean(x * x, axis=-1, keepdims=True)
    n = x * lax.rsqrt(ms + NORM_EPS) * g
    return n * (1.0 + scale) + shift


def _dot(a, b):
    return jnp.dot(a, b, preferred_element_type=F32)


def _dot_hi(a, b):
    return jnp.dot(a, b, preferred_element_type=F32, precision=HI)


def _iota_div(shape, axis, n):
    assert n & (n - 1) == 0
    return lax.shift_right_logical(lax.broadcasted_iota(jnp.int32, shape, axis), n.bit_length() - 1)


def _iota_mod(shape, axis, n):
    assert n & (n - 1) == 0
    return lax.broadcasted_iota(jnp.int32, shape, axis) & (n - 1)


def _head_ones(width, head):
    r = _iota_div((width, width), 0, head)
    c = _iota_div((width, width), 1, head)
    return jnp.where(r == c, 1.0, 0.0).astype(BF16)


def _head_sum(x, ones):
    hi = x.astype(BF16)
    lo = (x - hi.astype(F32)).astype(BF16)
    rows = x.shape[0]
    s = _dot(jnp.concatenate([hi, lo], axis=0), ones)
    return s[:rows] + s[rows:]


def _modulation_body(c_ref, w_ref, b_ref, o_ref):
    s = _silu(c_ref[...])
    o_ref[0] = _dot_hi(s, w_ref[0]) + b_ref[0]


def _modulation(cond, mod_w, mod_b):
    depth, d, d3 = mod_w.shape
    rows = cond.shape[0]
    return pl.pallas_call(
        _modulation_body,
        grid=(depth, d3 // d),
        in_specs=[
            pl.BlockSpec((rows, d), lambda i, j: (0, 0)),
            pl.BlockSpec((1, d, d), lambda i, j: (i, 0, j)),
            pl.BlockSpec((1, 1, d), lambda i, j: (i, 0, j)),
        ],
        out_specs=pl.BlockSpec((1, rows, d), lambda i, j: (i, 0, j)),
        out_shape=jax.ShapeDtypeStruct((depth, rows, d3), F32),
        compiler_params=_cparams("parallel", "parallel"),
        name="modulation",
    )(cond, mod_w, mod_b.reshape(depth, 1, d3))


def _conv_layer_body(xp_ref, xc_ref, xn_ref, g_ref, sh_ref, sc_ref, w3_ref, dw_ref, db_ref, lg_ref, lb_ref,
                     w_ref, gate_ref, *rest, width, final, tn):
    if final:
        fg_ref, o_ref, acc_ref, sg_ref = rest
    else:
        o_ref, acc_ref, sg_ref = rest
    i = pl.program_id(1)
    nt = pl.num_programs(1)
    tq, d = xc_ref.shape[1], xc_ref.shape[2]
    g, sh, sc = g_ref[...], sh_ref[0], sc_ref[0]
    xw = jnp.concatenate([xp_ref[0], xc_ref[0], xn_ref[0]], axis=0)
    h = _norm_mod(xw, g, sh, sc).astype(BF16)
    hc = h[HALO:HALO + tq]
    rows = lax.broadcasted_iota(jnp.int32, (tq + 2 * HALO, tn), 0)
    inside = ((rows >= HALO) | (i > 0)) & ((rows < HALO + tq) | (i < nt - 1))
    off = HALO - width // 2

    def project(n):
        cols = slice(n * tn, (n + 1) * tn)
        a = _dot(h, w3_ref[0, :, cols])
        b = _dot(h, w3_ref[1, :, cols])
        gt = _dot(hc, w3_ref[2, :, cols])
        return jnp.where(inside, a * _sigmoid(b), 0.0), gt

    def conv(n, y, gt):
        cols = slice(n * tn, (n + 1) * tn)
        acc = db_ref[:, cols]
        for res in range(SUBLANES):
            part = None
            for k in range(width):
                if (off + k) % SUBLANES != res:
                    continue
                base = off + k - res
                term = y[base:base + tq + SUBLANES] * dw_ref[k:k + 1, cols]
                part = term if part is None else part + term
            if part is not None:
                acc = acc + part[res:res + tq]
        acc_ref[:, cols] = acc
        sg_ref[:, cols] = _silu(gt)

    nblk = d // tn
    nxt = project(0)
    for n in range(nblk):
        cur = nxt
        if n + 1 < nblk:
            nxt = project(n + 1)
        conv(n, *cur)
    acc = acc_ref[...]
    mean = jnp.mean(acc, axis=-1, keepdims=True)
    cen = acc - mean
    var = jnp.mean(cen * cen, axis=-1, keepdims=True)
    yn = cen * lax.rsqrt(var + LN_EPS) * lg_ref[...] + lb_ref[...]
    u = (_silu(yn) * sg_ref[...]).astype(BF16)
    out = xc_ref[0] + gate_ref[0] * _dot(u, w_ref[...])
    if final:
        ms = jnp.mean(out * out, axis=-1, keepdims=True)
        out = out * lax.rsqrt(ms + NORM_EPS) * fg_ref[...]
    o_ref[0] = out


def _conv_layer(x, g, shift, scale, w3, dw, db, ln_g, ln_b, w_out, gate, final_g, *, tq):
    bsz, t, d = x.shape
    width = dw.shape[0]
    assert width // 2 <= HALO and t % HALO == 0 and tq % HALO == 0
    r = tq // HALO
    nh = t // HALO
    tn = min(d, 256)
    row = pl.BlockSpec((1, tq, d), lambda b, i: (b, i, 0))
    prev = pl.BlockSpec((1, HALO, d), lambda b, i: (b, jnp.maximum(i * r - 1, 0), 0))
    nxt = pl.BlockSpec((1, HALO, d), lambda b, i: (b, jnp.minimum((i + 1) * r, nh - 1), 0))
    final = final_g is not None
    in_specs = [prev, row, nxt, _const_spec((1, d)), _mod_spec(shift), _mod_spec(scale), _const_spec((3, d, d)),
                _const_spec((width, d)), _const_spec((1, d)), _const_spec((1, d)), _const_spec((1, d)),
                _const_spec((d, d)), _mod_spec(gate)]
    args = [x, x, x, g, shift, scale, w3, dw, db, ln_g, ln_b, w_out, gate]
    if final:
        in_specs.append(_const_spec((1, d)))
        args.append(final_g)
    return pl.pallas_call(
        functools.partial(_conv_layer_body, width=width, final=final, tn=tn),
        grid=(bsz, t // tq),
        in_specs=in_specs,
        out_specs=row,
        out_shape=jax.ShapeDtypeStruct((bsz, t, d), F32),
        scratch_shapes=[pltpu.VMEM((tq, d), F32), pltpu.VMEM((tq, d), F32)],
        compiler_params=_cparams("parallel", "parallel"),
        name="conv_layer",
    )(*args)


def _rwkv_feat_body(xp_ref, xc_ref, xn_ref, g_ref, sh_ref, sc_ref, mu_ref, wr_ref, wk_ref, wv_ref, wg_ref,
                    w1_ref, w2_ref, a1_ref, a2_ref, w0_ref, a0_ref, kk_ref_in, ka_ref,
                    r_ref, v_ref, g_out_ref, kk_ref, lw0_ref, lw1_ref, b0_ref, b1_ref, kd0_ref, kd1_ref):
    i = pl.program_id(1)
    nt = pl.num_programs(1)
    tm, d = xc_ref.shape[1], xc_ref.shape[2]
    g, sh, sc = g_ref[...], sh_ref[0], sc_ref[0]
    h = _norm_mod(xc_ref[0], g, sh, sc)
    hp = _norm_mod(xp_ref[0], g, sh, sc)[7:8]
    hn = _norm_mod(xn_ref[0], g, sh, sc)[0:1]
    hp = jnp.where(i > 0, hp, 0.0)
    hn = jnp.where(i < nt - 1, hn, 0.0)
    rows = lax.broadcasted_iota(jnp.int32, (tm, d), 0)
    hm1 = jnp.where(rows == 0, hp, pltpu.roll(h, 1, 0))
    hp1 = jnp.where(rows == tm - 1, hn, pltpu.roll(h, tm - 1, 0))
    xx = 0.5 * (hm1 + hp1) - h

    def lerp(n):
        return (h + xx * mu_ref[n:n + 1, :]).astype(BF16)

    r = _dot(lerp(0), wr_ref[...])
    k = _dot(lerp(2), wk_ref[...])
    v = _dot(lerp(3), wv_ref[...])
    r_ref[0] = r.astype(BF16)
    v_ref[0] = v.astype(BF16)
    g_out_ref[0] = _dot(lerp(5), wg_ref[...]).astype(BF16)
    kx = k * kk_ref_in[...]
    ones = _head_ones(LANES, RWKV_HEAD)
    parts = []
    for c in range(d // LANES):
        sl = slice(c * LANES, (c + 1) * LANES)
        q = kx[:, sl]
        parts.append(q * lax.rsqrt(_head_sum(q * q, ones) + 1e-12))
    kk = parts[0] if len(parts) == 1 else jnp.concatenate(parts, axis=1)
    kk_ref[0] = kk.astype(BF16)
    tw = jnp.tanh(_dot(lerp(1), w1_ref[...])).astype(BF16)
    ta = _dot(lerp(4), a1_ref[...]).astype(BF16)
    for dr, (lw_ref, b_ref, kd_ref) in enumerate(((lw0_ref, b0_ref, kd0_ref), (lw1_ref, b1_ref, kd1_ref))):
        z = w0_ref[dr:dr + 1, :] + _dot(tw, w2_ref[dr])
        sp = jnp.maximum(-z, 0.0) + jnp.log(1.0 + jnp.exp(-jnp.abs(z)))
        lw_ref[0] = -jnp.exp(-sp - 0.5)
        a = _sigmoid(a0_ref[dr:dr + 1, :] + _dot(ta, a2_ref[dr]))
        b_ref[0] = (kk * a).astype(BF16)
        kd_ref[0] = (k * (1.0 + (a - 1.0) * ka_ref[...])).astype(BF16)


def _rwkv_feat(x, g, shift, scale, p, *, tm):
    bsz, t, d = x.shape
    assert t % 8 == 0 and tm % 8 == 0
    r8 = tm // 8
    n8 = t // 8
    row = pl.BlockSpec((1, tm, d), lambda b, i: (b, i, 0))
    prev = pl.BlockSpec((1, 8, d), lambda b, i: (b, jnp.maximum(i * r8 - 1, 0), 0))
    nxt = pl.BlockSpec((1, 8, d), lambda b, i: (b, jnp.minimum((i + 1) * r8, n8 - 1), 0))
    lora = p["w1"].shape[-1]
    in_specs = [prev, row, nxt, _const_spec((1, d)), _mod_spec(shift), _mod_spec(scale), _const_spec((6, d)),
                _const_spec((d, d)), _const_spec((d, d)), _const_spec((d, d)), _const_spec((d, d)),
                _const_spec((d, lora)), _const_spec((2, lora, d)), _const_spec((d, lora)), _const_spec((2, lora, d)),
                _const_spec((2, d)), _const_spec((2, d)), _const_spec((1, d)), _const_spec((1, d))]
    return pl.pallas_call(
        _rwkv_feat_body,
        grid=(bsz, t // tm),
        in_specs=in_specs,
        out_specs=[row] * 10,
        out_shape=[jax.ShapeDtypeStruct((bsz, t, d), dt) for dt in [BF16] * 4 + [F32] * 2 + [BF16] * 4],
        compiler_params=_cparams("parallel", "parallel"),
        name="rwkv_feat",
    )(x, x, x, g, shift, scale, p["mu"], p["w_r"], p["w_k"], p["w_v"], p["w_g"],
      p["w1"], p["w2"], p["a1"], p["a2"], p["w0"], p["a0"], p["k_k"], p["k_a"])


def _scan_body(lw_ref, kk_ref, b_ref, kd_ref, v_ref, r_ref, s0_ref, o_ref, s1_ref, h_ref, *, reverse, gl):
    c = pl.program_id(1)
    nc = pl.num_programs(1)
    nb, cs_len, d = lw_ref.shape
    n = RWKV_HEAD
    assert cs_len == n, "block-diagonal packing assumes chunk == head size"
    hp = gl // n

    @pl.when(c == 0)
    def _():
        h_ref[...] = s0_ref[...]

    ri = lax.broadcasted_iota(jnp.int32, (cs_len, cs_len), 0)
    ci = lax.broadcasted_iota(jnp.int32, (cs_len, cs_len), 1)
    tri = jnp.where((ci >= ri) if reverse else (ci <= ri), 1.0, 0.0).astype(BF16)
    kkt, kh, bh, rt, kb, bb, vv, g_tot = ([] for _ in range(8))
    for bi in range(nb):
        lw = lw_ref[bi]
        lw_hi = lw.astype(BF16)
        rem = lw - lw_hi.astype(F32)
        lw_mid = rem.astype(BF16)
        lw_lo = (rem - lw_mid.astype(F32)).astype(BF16)
        cum = _dot(tri, lw_hi) + _dot(tri, lw_mid) + _dot(tri, lw_lo)
        tot = cum[0:1] if reverse else cum[cs_len - 1:cs_len]
        g_inv = jnp.exp(-cum)
        g_tot.append(jnp.exp(tot))
        kkt.append(kk_ref[bi].astype(F32) * jnp.exp(cum - lw))
        kh.append(kd_ref[bi].astype(F32) * g_inv)
        bh.append(b_ref[bi].astype(F32) * g_inv)
        rt.append(r_ref[bi].astype(F32) * jnp.exp(cum))
        kb.append(kh[bi] * g_tot[bi])
        bb.append(bh[bi] * g_tot[bi])
        vv.append(v_ref[bi].astype(F32))

    bdmask = _iota_div((gl, gl), 0, n) == _iota_div((gl, gl), 1, n)
    i_s = lax.broadcasted_iota(jnp.int32, (cs_len, gl), 0)
    j_s = _iota_mod((cs_len, gl), 1, n)
    strict = (j_s > i_s) if reverse else (j_s < i_s)
    incl = (j_s >= i_s) if reverse else (j_s <= i_s)
    eye = j_s == i_s
    shift = INV_BLOCK.bit_length() - 1
    dblock = lax.shift_right_logical(i_s, shift) == lax.shift_right_logical(j_s, shift)
    lane_head = _iota_div((n, gl), 1, n)

    def bd(y):
        return jnp.where(bdmask, jnp.concatenate([y] * hp, axis=0), 0.0).astype(BF16)

    def mm(a, bm):
        return _dot(a.astype(BF16), bm)

    def mm_t(a, bm):
        return lax.dot_general(a.astype(BF16), bm, (((1,), (1,)), ((), ())), preferred_element_type=F32)

    def collapse(m):
        out = jnp.where(lane_head == 0, m[0:n], 0.0)
        for hh in range(1, hp):
            out = out + jnp.where(lane_head == hh, m[hh * n:(hh + 1) * n], 0.0)
        return out

    groups = [(bi, slice(g * gl, (g + 1) * gl)) for bi in range(nb) for g in range(d // gl)]
    cat0 = functools.partial(jnp.concatenate, axis=0)
    lhs = [cat0([kkt[bi][:, sl], rt[bi][:, sl]]) for bi, sl in groups]
    ab = [mm_t(l, bd(bh[bi][:, sl])) for l, (bi, sl) in zip(lhs, groups)]
    ak = [mm_t(l, bd(kh[bi][:, sl])) for l, (bi, sl) in zip(lhs, groups)]
    a_m = [jnp.where(strict, y[:cs_len], 0.0) for y in ab]
    m2 = [jnp.where(incl, y[cs_len:], 0.0) for y in ab]
    b_m = [jnp.where(strict, y[:cs_len], 0.0) for y in ak]
    m1 = [jnp.where(incl, y[cs_len:], 0.0) for y in ak]
    ident = jnp.where(eye, 1.0, 0.0)
    a_d = [jnp.where(dblock, y, 0.0) for y in a_m]
    xp = [-y for y in a_d]
    tm_ = [ident + y for y in xp]
    xp = [mm(y, bd(y)) for y in xp]
    for _ in range(int(math.log2(INV_BLOCK)) - 2):
        st = [mm(cat0([y, z]), bd(y)) for y, z in zip(xp, tm_)]
        xp = [y[:cs_len] for y in st]
        tm_ = [z + y[cs_len:] for y, z in zip(st, tm_)]
    t_d = [z + mm(z, bd(y)) for y, z in zip(xp, tm_)]
    nn = [mm(z, bd(y - w)) for z, y, w in zip(t_d, a_m, a_d)]
    vm = [ident - y for y in nn]
    for _ in range(int(math.log2(cs_len // INV_BLOCK)) - 1):
        nn = [mm(y, bd(y)) for y in nn]
        vm = [z + mm(z, bd(y)) for y, z in zip(nn, vm)]
    tm_ = [mm(z, bd(y)) for y, z in zip(t_d, vm)]
    wt = [mm(z, bd(kkt[bi][:, sl])) for z, (bi, sl) in zip(tm_, groups)]
    sv = [mm(cat0([y, z]), bd(vv[bi][:, sl])) for y, z, (bi, sl) in zip(b_m, m1, groups)]
    u0 = [mm(z, bd(y[:cs_len])) for y, z in zip(sv, tm_)]
    rbar = [rt[bi][:, sl] - mm(y, bd(z)) for y, z, (bi, sl) in zip(m2, wt, groups)]
    o0 = [y[cs_len:] - mm(z, bd(w)) for y, z, w in zip(sv, m2, u0)]
    bbt = [bb[bi][:, sl].T for bi, sl in groups]
    kbt = [kb[bi][:, sl].T for bi, sl in groups]
    pc = [mm(y, z.astype(BF16)) for y, z in zip(bbt, wt)]
    gc = [mm(jnp.concatenate([y, -z], axis=1), cat0([vv[bi][:, sl], w]).astype(BF16))
          for y, z, w, (bi, sl) in zip(kbt, bbt, u0, groups)]
    p_s = [jnp.where(eye, g_tot[bi][:, sl], 0.0) - collapse(y) for y, (bi, sl) in zip(pc, groups)]
    g_s = [collapse(y) for y in gc]
    so = [mm(cat0([y, z]), bd(h_ref[bi, :, sl])) for y, z, (bi, sl) in zip(rbar, p_s, groups)]
    for y, z, w, (bi, sl) in zip(so, o0, g_s, groups):
        o_ref[bi, :, sl] = y[:cs_len] + z
        h_ref[bi, :, sl] = y[cs_len:] + w

    @pl.when(c == nc - 1)
    def _():
        s1_ref[...] = h_ref[...]


def _rwkv_scan(lw, kk, b, kd, v, r, state, *, reverse):
    bsz, t, d = lw.shape
    cs = SCAN_CHUNK
    nc = t // cs
    gl = min(d, 256)
    nb = math.gcd(bsz, SCAN_BATCH)
    if reverse:
        row = pl.BlockSpec((nb, cs, d), lambda bi, c: (bi, nc - 1 - c, 0))
    else:
        row = pl.BlockSpec((nb, cs, d), lambda bi, c: (bi, c, 0))
    st = pl.BlockSpec((nb, RWKV_HEAD, d), lambda bi, c: (bi, 0, 0))
    return pl.pallas_call(
        functools.partial(_scan_body, reverse=reverse, gl=gl),
        grid=(bsz // nb, nc),
        in_specs=[row] * 6 + [st],
        out_specs=[row, st],
        out_shape=[jax.ShapeDtypeStruct((bsz, t, d), F32), jax.ShapeDtypeStruct((bsz, RWKV_HEAD, d), F32)],
        scratch_shapes=[pltpu.VMEM((nb, RWKV_HEAD, d), F32)],
        compiler_params=_cparams("parallel", "arbitrary"),
        name="rwkv_scan_rev" if reverse else "rwkv_scan_fwd",
    )(lw, kk, b, kd, v, r, state)


def _rwkv_out_body(o0_ref, o1_ref, r_ref, kd0_ref, kd1_ref, v_ref, g_ref, x_ref, rk_ref, lg_ref, lb_ref, w_ref,
                   gate_ref, out_ref, y_ref):
    d = x_ref.shape[-1]
    ones = _head_ones(LANES, RWKV_HEAD)
    inv_n = 1.0 / RWKV_HEAD
    for c in range(d // LANES):
        sl = slice(c * LANES, (c + 1) * LANES)
        o = o0_ref[0, :, sl] + o1_ref[0, :, sl]
        mean = _head_sum(o, ones) * inv_n
        cen = o - mean
        var = _head_sum(cen * cen, ones) * inv_n
        on = cen * lax.rsqrt(var + GN_EPS) * lg_ref[:, sl] + lb_ref[:, sl]
        ksum = kd0_ref[0, :, sl].astype(F32) + kd1_ref[0, :, sl].astype(F32)
        bonus = _head_sum(r_ref[0, :, sl].astype(F32) * ksum * rk_ref[:, sl], ones) * v_ref[0, :, sl].astype(F32)
        y_ref[:, sl] = ((on + bonus) * _silu(g_ref[0, :, sl].astype(F32))).astype(BF16)
    out_ref[0] = x_ref[0] + gate_ref[0] * _dot(y_ref[...], w_ref[...])


def _rwkv_out(o0, o1, r, kd0, kd1, v, g, x, r_k, ln_g, ln_b, w_o, gate, *, tm):
    bsz, t, d = x.shape
    row = pl.BlockSpec((1, tm, d), lambda b, i: (b, i, 0))
    return pl.pallas_call(
        _rwkv_out_body,
        grid=(bsz, t // tm),
        in_specs=[row] * 8 + [_const_spec((1, d))] * 3 + [_const_spec((d, d)), _mod_spec(gate)],
        out_specs=row,
        out_shape=jax.ShapeDtypeStruct((bsz, t, d), F32),
        scratch_shapes=[pltpu.VMEM((tm, d), BF16)],
        compiler_params=_cparams("parallel", "parallel"),
        name="rwkv_out",
    )(o0, o1, r, kd0, kd1, v, g, x, r_k, ln_g, ln_b, w_o, gate)


def _head_norm(x, g):
    ms = jnp.mean(x * x, axis=-1, keepdims=True)
    return x * lax.rsqrt(ms + NORM_EPS) * g


def _rope(x, cos, sin_signed):
    lane = _iota_mod(x.shape, 1, ATTN_HEAD // 2)
    swapped = jnp.where(lane < ATTN_HEAD // 4, pltpu.roll(x, ATTN_HEAD - ATTN_HEAD // 4, 1),
                        pltpu.roll(x, ATTN_HEAD // 4, 1))
    return x * cos + swapped * sin_signed


def _attn_in_body(x_ref, g_ref, sh_ref, sc_ref, w_ref, qg_ref, kg_ref, cos_ref, sin_ref, *outs,
                  qw, kvw, with_q, rope):
    h = _norm_mod(x_ref[0], g_ref[...], sh_ref[0], sc_ref[0]).astype(BF16)
    hd = ATTN_HEAD
    if with_q:
        q_ref, k_ref, v_ref, gt_ref = outs
        base_k = qw
    else:
        k_ref, v_ref = outs
        base_k = 0
    cos, sin = cos_ref[...], sin_ref[...]
    tn = 2 * hd
    if with_q:
        for n in range(qw // tn):
            qq = _dot(h, w_ref[:, n * tn:(n + 1) * tn])
            for half in range(2):
                q = _head_norm(qq[:, half * hd:(half + 1) * hd], qg_ref[...])
                q_ref[0, :, n * tn + half * hd:n * tn + (half + 1) * hd] = (
                    _rope(q, cos, sin) * (hd ** -0.5)).astype(BF16)
    for n in range(kvw // tn):
        kk2 = _dot(h, w_ref[:, base_k + n * tn:base_k + (n + 1) * tn])
        for half in range(2):
            k = _head_norm(kk2[:, half * hd:(half + 1) * hd], kg_ref[...])
            if rope:
                k = _rope(k, cos, sin)
            k_ref[0, :, n * tn + half * hd:n * tn + (half + 1) * hd] = k.astype(BF16)
    for n in range(kvw // tn):
        v_ref[0, :, n * tn:(n + 1) * tn] = _dot(
            h, w_ref[:, base_k + kvw + n * tn:base_k + kvw + (n + 1) * tn]).astype(BF16)
    if with_q:
        for n in range(qw // tn):
            gt = _dot(h, w_ref[:, qw + 2 * kvw + n * tn:qw + 2 * kvw + (n + 1) * tn])
            gt_ref[0, :, n * tn:(n + 1) * tn] = _silu(gt).astype(BF16)


def _attn_in(x, g, shift, scale, w, q_g, k_g, cos, sin, *, qw, kvw, with_q, rope, tm):
    bsz, t, d = x.shape
    row = pl.BlockSpec((1, tm, d), lambda b, i: (b, i, 0))
    tab = pl.BlockSpec((tm, ATTN_HEAD), lambda b, i: (i, 0))
    nw = w.shape[1]
    outs = []
    shapes = []
    if with_q:
        outs.append(pl.BlockSpec((1, tm, qw), lambda b, i: (b, i, 0)))
        shapes.append(jax.ShapeDtypeStruct((bsz, t, qw), BF16))
    for _ in range(2):
        outs.append(pl.BlockSpec((1, tm, kvw), lambda b, i: (b, i, 0)))
        shapes.append(jax.ShapeDtypeStruct((bsz, t, kvw), BF16))
    if with_q:
        outs.append(pl.BlockSpec((1, tm, qw), lambda b, i: (b, i, 0)))
        shapes.append(jax.ShapeDtypeStruct((bsz, t, qw), BF16))
    return pl.pallas_call(
        functools.partial(_attn_in_body, qw=qw, kvw=kvw, with_q=with_q, rope=rope),
        grid=(bsz, t // tm),
        in_specs=[row, _const_spec((1, d)), _mod_spec(shift), _mod_spec(scale), _const_spec((d, nw)),
                  _const_spec((1, ATTN_HEAD)), _const_spec((1, ATTN_HEAD)), tab, tab],
        out_specs=outs,
        out_shape=shapes,
        compiler_params=_cparams("parallel", "parallel"),
        name="attn_in" if with_q else "attn_in_ctx",
    )(x, g, shift, scale, w, q_g, k_g, cos, sin)


def _flash_body(q_ref, kc_ref, k_ref, vc_ref, v_ref, o_ref, *, sub):
    k = jnp.concatenate([kc_ref[0], k_ref[0]], axis=0)
    v = jnp.concatenate([vc_ref[0], v_ref[0]], axis=0)
    vt = v.astype(F32).T.astype(BF16)
    tq = q_ref.shape[1]
    units = [(slice(r, r + sub), slice(gq * ATTN_HEAD, (gq + 1) * ATTN_HEAD))
             for r in range(0, tq, sub) for gq in range(ATTN_GROUP)]
    nt = (((1,), (1,)), ((), ()))
    st = [lax.dot_general(k, q_ref[0, rs, sl], nt, preferred_element_type=F32) for rs, sl in units]
    m = [jnp.max(s, axis=0, keepdims=True) for s in st]
    p = [jnp.exp(s - mx) for s, mx in zip(st, m)]
    den = [jnp.sum(y, axis=0, keepdims=True) for y in p]
    ot = [_dot(vt, y.astype(BF16)) for y in p]
    for (rs, sl), y, dn in zip(units, ot, den):
        o_ref[0, rs, sl] = (y / dn).T.astype(BF16)


def _flash(q, kc, k, vc, v, *, tq):
    bsz, t, qw = q.shape
    lc = kc.shape[1]
    hkv = k.shape[2] // ATTN_HEAD
    gw = ATTN_GROUP * ATTN_HEAD
    qspec = pl.BlockSpec((1, tq, gw), lambda b, hh, i: (b, i, hh))
    cspec = pl.BlockSpec((1, lc, ATTN_HEAD), lambda b, hh, i: (b, 0, hh))
    kspec = pl.BlockSpec((1, t, ATTN_HEAD), lambda b, hh, i: (b, 0, hh))
    return pl.pallas_call(
        functools.partial(_flash_body, sub=min(tq, 256)),
        grid=(bsz, hkv, t // tq),
        in_specs=[qspec, cspec, kspec, cspec, kspec],
        out_specs=qspec,
        out_shape=jax.ShapeDtypeStruct((bsz, t, qw), BF16),
        compiler_params=_cparams("parallel", "parallel", "arbitrary"),
        name="flash_gqa",
    )(q, kc, k, vc, v)


def _gated_out_body(o_ref, sg_ref, x_ref, w_ref, gate_ref, out_ref):
    u = (o_ref[0].astype(F32) * sg_ref[0].astype(F32)).astype(BF16)
    out_ref[0] = x_ref[0] + gate_ref[0] * _dot(u, w_ref[...])


def _gated_out(o, sg, x, w, gate, *, tm):
    bsz, t, d = x.shape
    kdim = o.shape[-1]
    row = pl.BlockSpec((1, tm, d), lambda b, i: (b, i, 0))
    wide = pl.BlockSpec((1, tm, kdim), lambda b, i: (b, i, 0))
    return pl.pallas_call(
        _gated_out_body,
        grid=(bsz, t // tm),
        in_specs=[wide, wide, row, _const_spec((kdim, d)), _mod_spec(gate)],
        out_specs=row,
        out_shape=jax.ShapeDtypeStruct((bsz, t, d), F32),
        compiler_params=_cparams("parallel", "parallel"),
        name="attn_out",
    )(o, sg, x, w, gate)


def _final_norm_body(x_ref, g_ref, o_ref):
    x = x_ref[0]
    ms = jnp.mean(x * x, axis=-1, keepdims=True)
    o_ref[0] = x * lax.rsqrt(ms + NORM_EPS) * g_ref[...]


def _final_norm(x, g, *, tm):
    bsz, t, d = x.shape
    row = pl.BlockSpec((1, tm, d), lambda b, i: (b, i, 0))
    return pl.pallas_call(
        _final_norm_body,
        grid=(bsz, t // tm),
        in_specs=[row, _const_spec((1, d))],
        out_specs=row,
        out_shape=jax.ShapeDtypeStruct((bsz, t, d), F32),
        compiler_params=_cparams("parallel", "parallel"),
        name="final_norm",
    )(x, g)


def _rope_tables(t):
    axis_dim = ATTN_HEAD // 2
    rows = t // GRID_W
    row = jnp.repeat(jnp.arange(rows), GRID_W).astype(F32)
    col = (jnp.arange(rows * GRID_W) % GRID_W).astype(F32)
    inv = 1.0 / (ROPE_THETA ** (jnp.arange(0, axis_dim, 2, dtype=F32) / axis_dim))
    ar, ac = row[:, None] * inv, col[:, None] * inv
    cos = jnp.concatenate([jnp.cos(ar), jnp.cos(ar), jnp.cos(ac), jnp.cos(ac)], axis=-1)
    sin = jnp.concatenate([-jnp.sin(ar), jnp.sin(ar), -jnp.sin(ac), jnp.sin(ac)], axis=-1)
    return cos, sin


def _tile(t, want):
    tm = min(t, want)
    assert t % tm == 0
    return tm


def kernel(x, c, ctx, c_ctx, norm_g, mod_w, mod_b, conv_w_in, conv_dw, conv_db, conv_ln_g, conv_ln_b, conv_w_out, rwkv_mu, rwkv_w_r, rwkv_w_k, rwkv_w_v, rwkv_w_g, rwkv_w0, rwkv_w1, rwkv_w2, rwkv_a0, rwkv_a1, rwkv_a2, rwkv_k_k, rwkv_k_a, rwkv_r_k, rwkv_ln_g, rwkv_ln_b, rwkv_w_o, attn_w_in, attn_q_g, attn_k_g, attn_w_out, final_g):
    bsz, t, d = x.shape
    lc = ctx.shape[1]
    depth = mod_w.shape[0]
    qw = (d // 64) * ATTN_HEAD
    kvw = qw // ATTN_GROUP

    pad = (-(bsz + 1)) % 8
    cond = jnp.concatenate([c, c_ctx[None, :], jnp.zeros((pad, d), F32)], axis=0)
    mods = _modulation(cond, mod_w, mod_b)
    cos, sin = _rope_tables(t)
    final_row = final_g.reshape(1, d)

    xc = ctx
    for i in range(depth):
        kind, j = i % N_MIXERS, i // N_MIXERS
        ctx_out = any(l % N_MIXERS != 0 for l in range(i + 1, depth))
        ctx_in = ctx_out or kind != 0
        last = i == depth - 1
        m = mods[i]
        shift, scale, gate = (m[:bsz, None, k * d:(k + 1) * d] for k in range(3))
        shift_c, scale_c, gate_c = (m[bsz:bsz + 1, None, k * d:(k + 1) * d] for k in range(3))
        g = norm_g[i].reshape(1, d)
        if kind == 0:
            w3 = conv_w_in[j].reshape(d, 3, d).transpose(1, 0, 2).astype(BF16)
            w_out = conv_w_out[j].astype(BF16)
            cp = (conv_dw[j], conv_db[j].reshape(1, d), conv_ln_g[j].reshape(1, d), conv_ln_b[j].reshape(1, d), w_out)
            x_new = _conv_layer(x, g, shift, scale, w3, *cp, gate, final_row if last else None, tq=_tile(t, ROW_TILE))
            if ctx_out:
                xc = _conv_layer(xc, g, shift_c, scale_c, w3, *cp, gate_c, None, tq=_tile(lc, ROW_TILE))
            x = x_new
        elif kind == 1:
            lora = rwkv_w1.shape[-1]
            zero = jnp.zeros((lora, d), F32)

            def padded(w2):
                return jnp.stack([jnp.concatenate([w2[0], zero], 0), jnp.concatenate([zero, w2[1]], 0)]).astype(BF16)

            p = dict(
                mu=rwkv_mu[j], w_r=rwkv_w_r[j].astype(BF16), w_k=rwkv_w_k[j].astype(BF16),
                w_v=rwkv_w_v[j].astype(BF16), w_g=rwkv_w_g[j].astype(BF16),
                w1=jnp.concatenate([rwkv_w1[j, 0], rwkv_w1[j, 1]], axis=-1).astype(BF16), w2=padded(rwkv_w2[j]),
                a1=jnp.concatenate([rwkv_a1[j, 0], rwkv_a1[j, 1]], axis=-1).astype(BF16), a2=padded(rwkv_a2[j]),
                w0=rwkv_w0[j], a0=rwkv_a0[j], k_k=rwkv_k_k[j].reshape(1, d), k_a=rwkv_k_a[j].reshape(1, d))
            p["w1"] = p["w1"].reshape(d, 2 * lora)
            fl = _rwkv_feat(x, g, shift, scale, p, tm=_tile(t, ROW_TILE))
            fc = _rwkv_feat(xc, g, shift_c, scale_c, p, tm=_tile(lc, ROW_TILE))
            state0 = jnp.zeros((bsz, RWKV_HEAD, d), F32)
            o_l, o_c = [], []
            for dr, rev in ((0, False), (1, True)):
                oc_d, st = _rwkv_scan(fc[4 + dr], fc[3], fc[6 + dr], fc[8 + dr], fc[1], fc[0], state0, reverse=rev)
                ol_d, _ = _rwkv_scan(fl[4 + dr], fl[3], fl[6 + dr], fl[8 + dr], fl[1], fl[0], st, reverse=rev)
                o_l.append(ol_d)
                o_c.append(oc_d)
            ro = (rwkv_r_k[j].reshape(1, d), rwkv_ln_g[j].reshape(1, d), rwkv_ln_b[j].reshape(1, d),
                  rwkv_w_o[j].astype(BF16))
            x_new = _rwkv_out(o_l[0], o_l[1], fl[0], fl[8], fl[9], fl[1], fl[2], x, *ro, gate, tm=_tile(t, ROW_TILE))
            if ctx_out:
                xc = _rwkv_out(o_c[0], o_c[1], fc[0], fc[8], fc[9], fc[1], fc[2], xc, *ro, gate_c, tm=_tile(lc, ROW_TILE))
            x = x_new
        else:
            w_in = attn_w_in[j].astype(BF16)
            qg, kg = attn_q_g[j].reshape(1, ATTN_HEAD), attn_k_g[j].reshape(1, ATTN_HEAD)
            q, k, v, sg = _attn_in(x, g, shift, scale, w_in, qg, kg, cos, sin, qw=qw, kvw=kvw, with_q=True,
                                   rope=True, tm=_tile(t, ROW_TILE))
            if ctx_out:
                raise NotImplementedError("context-stream attention output is not needed at this depth")
            no_rope = jnp.zeros((lc, ATTN_HEAD), F32)
            kc, vc = _attn_in(xc, g, shift_c, scale_c, w_in[:, qw:qw + 2 * kvw], qg, kg, no_rope, no_rope,
                              qw=qw, kvw=kvw, with_q=False, rope=False, tm=_tile(lc, ROW_TILE))
            o = _flash(q, kc, k, vc, v, tq=_tile(t, FLASH_Q_TILE))
            x = _gated_out(o, sg, x, attn_w_out[j].astype(BF16), gate, tm=_tile(t, ROW_TILE))
        if last and kind != 0:
            x = _final_norm(x, final_row, tm=_tile(t, ROW_TILE))
    return x
```
